```python
import jax, jax.numpy as jnp
from jax import lax
import numpy as np

D_MODEL = 1024
BATCH = 2
SEQ = 8192
DEPTH = 2

CHUNK = 64
LEFT_CHUNKS = 8
BAND = LEFT_CHUNKS + 1
N_HEADS = 16
HEAD_DIM = D_MODEL // N_HEADS
D_ATT = N_HEADS * HEAD_DIM
REL_CLIP = 128
N_REL = 2 * REL_CLIP + 1
D_RNN = ((4 * D_MODEL // 3 + 127) // 128) * 128
RNN_BLOCKS = 16
RNN_BLOCK_DIM = D_RNN // RNN_BLOCKS
CONV_WIDTH = 4
LRU_C = 8.0
D_FF = ((8 * D_MODEL // 3 + 127) // 128) * 128
EPS = 1e-6
IN_SPLITS = (D_RNN, 2 * D_RNN, 2 * D_RNN + D_ATT, 2 * D_RNN + 2 * D_ATT,
             2 * D_RNN + 3 * D_ATT, 2 * D_RNN + 3 * D_ATT + D_MODEL)
D_IN = 2 * D_RNN + 3 * D_ATT + 2 * D_MODEL

kernel_name = "hybrid_rglru_chunkattn_macaron"


def rms_norm(x, g):
    xf = x.astype(jnp.float32)
    ms = jnp.mean(xf * xf, axis=-1, keepdims=True)
    return (xf * lax.rsqrt(ms + EPS) * g.astype(jnp.float32)).astype(x.dtype)


def swiglu(h, w_gate, w_up, w_down):
    return (jax.nn.silu(h @ w_gate) * (h @ w_up)) @ w_down


def causal_depthwise_conv(x, w, b):
    S = x.shape[1]
    xp = jnp.pad(x, ((0, 0), (CONV_WIDTH - 1, 0), (0, 0)))
    out = b
    for j in range(CONV_WIDTH):
        out = out + xp[:, j:j + S] * w[j]
    return out


def block_diag_linear(x, w, b):
    B, S, _ = x.shape
    xb = x.reshape(B, S, RNN_BLOCKS, RNN_BLOCK_DIM)
    return jnp.einsum('bshi,hij->bshj', xb, w).reshape(B, S, D_RNN) + b


def rg_lru(x, w_a, b_a, w_x, b_x, lam):
    r = jax.nn.sigmoid(block_diag_linear(x, w_a, b_a).astype(jnp.float32))
    i = jax.nn.sigmoid(block_diag_linear(x, w_x, b_x).astype(jnp.float32))
    log_a = -LRU_C * r * jax.nn.softplus(-lam.astype(jnp.float32))
    a = jnp.exp(log_a)
    inp = jnp.sqrt(-jnp.expm1(2.0 * log_a)) * (i * x.astype(jnp.float32))

    def combine(left, right):
        a_l, b_l = left
        a_r, b_r = right
        return a_l * a_r, a_r * b_l + b_r

    _, h = lax.associative_scan(combine, (a, inp), axis=1)
    return h.astype(x.dtype)


def rel_bias_index():
    q_pos = np.arange(CHUNK)[:, None]
    k_pos = np.arange(BAND * CHUNK)[None, :] - LEFT_CHUNKS * CHUNK
    return np.clip(q_pos - k_pos, -REL_CLIP, REL_CLIP) + REL_CLIP


def chunked_attention(q, k, v, rel_table):
    B, S, H, Dh = q.shape
    N = S // CHUNK
    qc = q.reshape(B, N, CHUNK, H, Dh)
    pad = ((0, 0), (LEFT_CHUNKS, 0), (0, 0), (0, 0), (0, 0))
    kp = jnp.pad(k.reshape(B, N, CHUNK, H, Dh), pad)
    vp = jnp.pad(v.reshape(B, N, CHUNK, H, Dh), pad)
    band_idx = np.arange(N)[:, None] + np.arange(BAND)[None, :]
    kb = kp[:, band_idx].reshape(B, N, BAND * CHUNK, H, Dh)
    vb = vp[:, band_idx].reshape(B, N, BAND * CHUNK, H, Dh)
    valid = np.repeat(band_idx >= LEFT_CHUNKS, CHUNK, axis=1)
    bias = rel_table.astype(jnp.float32)[:, rel_bias_index()]
    s = jnp.einsum('bnqhd,bnkhd->bhnqk', qc, kb).astype(jnp.float32) * (HEAD_DIM ** -0.5)
    s = s + bias[None, :, None]
    s = jnp.where(valid[None, None, :, None, :], s, -1e30)
    p = jax.nn.softmax(s, axis=-1).astype(v.dtype)
    o = jnp.einsum('bhnqk,bnkhd->bnqhd', p, vb)
    return o.reshape(B, S, H * Dh)


def setup_inputs(seed: int = 0) -> dict:
    key = jax.random.key(seed)
    ks = jax.random.split(key, 32)
    nrm = lambda k, shape, scale: jax.random.normal(k, shape, jnp.float32) * scale
    gain = lambda k, shape: 1.0 + 0.02 * jax.random.normal(k, shape, jnp.float32)
    u = jax.random.uniform(ks[9], (DEPTH, D_RNN), jnp.float32, minval=0.9, maxval=0.999)
    a0 = u ** (1.0 / LRU_C)
    return {
        "x": nrm(ks[0], (BATCH, SEQ, D_MODEL), 1.0),
        "norm_ffn1": gain(ks[1], (DEPTH, D_MODEL)),
        "ffn1_w_gate": nrm(ks[2], (DEPTH, D_MODEL, D_FF), D_MODEL ** -0.5),
        "ffn1_w_up": nrm(ks[3], (DEPTH, D_MODEL, D_FF), D_MODEL ** -0.5),
        "ffn1_w_down": nrm(ks[4], (DEPTH, D_FF, D_MODEL), D_FF ** -0.5),
        "norm_mix": gain(ks[5], (DEPTH, D_MODEL)),
        "w_in": nrm(ks[6], (DEPTH, D_MODEL, D_IN), D_MODEL ** -0.5),
        "gate_bias": nrm(ks[7], (DEPTH, 2 * D_MODEL), 0.01),
        "conv_w": nrm(ks[8], (DEPTH, CONV_WIDTH, D_RNN), CONV_WIDTH ** -0.5),
        "conv_b": nrm(ks[10], (DEPTH, D_RNN), 0.01),
        "rg_w_a": nrm(ks[11], (DEPTH, RNN_BLOCKS, RNN_BLOCK_DIM, RNN_BLOCK_DIM), RNN_BLOCK_DIM ** -0.5),
        "rg_b_a": nrm(ks[12], (DEPTH, D_RNN), 0.01),
        "rg_w_x": nrm(ks[13], (DEPTH, RNN_BLOCKS, RNN_BLOCK_DIM, RNN_BLOCK_DIM), RNN_BLOCK_DIM ** -0.5),
        "rg_b_x": nrm(ks[14], (DEPTH, D_RNN), 0.01),
        "rg_lambda": jnp.log(a0) - jnp.log1p(-a0),
        "w_up_a": nrm(ks[15], (DEPTH, D_RNN, D_MODEL), D_RNN ** -0.5),
        "q_gain": gain(ks[16], (DEPTH, HEAD_DIM)),
        "k_gain": gain(ks[17], (DEPTH, HEAD_DIM)),
        "rel_table": nrm(ks[18], (DEPTH, N_HEADS, N_REL), 0.1),
        "w_up_b": nrm(ks[19], (DEPTH, D_ATT, D_MODEL), D_ATT ** -0.5),
        "w_out": nrm(ks[20], (DEPTH, D_MODEL, D_MODEL), D_MODEL ** -0.5),
        "norm_ffn2": gain(ks[21], (DEPTH, D_MODEL)),
        "ffn2_w_gate": nrm(ks[22], (DEPTH, D_MODEL, D_FF), D_MODEL ** -0.5),
        "ffn2_w_up": nrm(ks[23], (DEPTH, D_MODEL, D_FF), D_MODEL ** -0.5),
        "ffn2_w_down": nrm(ks[24], (DEPTH, D_FF, D_MODEL), D_FF ** -0.5),
    }


def reference(x, norm_ffn1, ffn1_w_gate, ffn1_w_up, ffn1_w_down, norm_mix, w_in, gate_bias,
              conv_w, conv_b, rg_w_a, rg_b_a, rg_w_x, rg_b_x, rg_lambda, w_up_a,
              q_gain, k_gain, rel_table, w_up_b, w_out,
              norm_ffn2, ffn2_w_gate, ffn2_w_up, ffn2_w_down):
    B, S, _ = x.shape
    for l in range(DEPTH):
        x = x + 0.5 * swiglu(rms_norm(x, norm_ffn1[l]), ffn1_w_gate[l], ffn1_w_up[l], ffn1_w_down[l])

        h = rms_norm(x, norm_mix[l])
        proj = h @ w_in[l]
        xr, yr, q, k, v, g_a, g_b = jnp.split(proj, IN_SPLITS, axis=-1)

        hr = rg_lru(causal_depthwise_conv(xr, conv_w[l], conv_b[l]),
                    rg_w_a[l], rg_b_a[l], rg_w_x[l], rg_b_x[l], rg_lambda[l])
        y_a = (jax.nn.gelu(yr) * hr) @ w_up_a[l]

        qh = rms_norm(q.reshape(B, S, N_HEADS, HEAD_DIM), q_gain[l])
        kh = rms_norm(k.reshape(B, S, N_HEADS, HEAD_DIM), k_gain[l])
        vh = v.reshape(B, S, N_HEADS, HEAD_DIM)
        y_b = chunked_attention(qh, kh, vh, rel_table[l]) @ w_up_b[l]

        gates = jnp.concatenate([g_a, g_b], axis=-1) + gate_bias[l]
        gate_a = jax.nn.sigmoid(gates[..., :D_MODEL])
        gate_b = jax.nn.sigmoid(gates[..., D_MODEL:])
        x = x + (gate_a * y_a + gate_b * y_b) @ w_out[l]

        x = x + 0.5 * swiglu(rms_norm(x, norm_ffn2[l]), ffn2_w_gate[l], ffn2_w_up[l], ffn2_w_down[l])
    return x
```

```python
import functools

import numpy as np
import jax
import jax.numpy as jnp
from jax import lax
from jax.experimental import pallas as pl
from jax.experimental.pallas import tpu as pltpu

F32 = jnp.float32
BF16 = jnp.bfloat16

LANES = 128
SUBLANES = 8
MXU_DIM = 256
MIB = 1024 * 1024

CHUNK = 64
LEFT_CHUNKS = 8
HEAD_DIM = 64
REL_CLIP = 128
CONV_WIDTH = 4
LRU_C = 8.0
EPS = 1e-6

TOKEN_TILE = 512
FF_TILE = 3 * MXU_DIM
TIME_TILE = 256
ATT_QB = 2 * CHUNK
ATT_W = ATT_QB + LEFT_CHUNKS * CHUNK
ATT_SUB = 4
ATT_HG = MXU_DIM // HEAD_DIM
ATT_NVAR = LEFT_CHUNKS * CHUNK // ATT_QB + 1


def _const_spec(shape):
    nd = len(shape)
    return pl.BlockSpec(shape, lambda *_: (0,) * nd, pipeline_mode=pl.Buffered(1))


def _rms_norm(x, g):
    ms = jnp.mean(x * x, axis=-1, keepdims=True)
    return x * lax.rsqrt(ms + EPS) * g


def _dot(a, b):
    return jnp.dot(a, b, preferred_element_type=F32)


def _ffn_residual(x, g_ref, wg_ref, wu_ref, wd_ref):
    d_ff = wg_ref.shape[1]
    h = _rms_norm(x, g_ref[...]).astype(BF16)
    acc = None
    for c0 in range(0, d_ff, FF_TILE):
        c1 = min(c0 + FF_TILE, d_ff)
        gate = _dot(h, wg_ref[:, c0:c1])
        up = _dot(h, wu_ref[:, c0:c1])
        act = (gate * jax.nn.sigmoid(gate) * up).astype(BF16)
        part = _dot(act, wd_ref[c0:c1, :])
        acc = part if acc is None else acc + part
    return x + 0.5 * acc


def _ffn_kernel(x_ref, g_ref, wg_ref, wu_ref, wd_ref, o_ref):
    o_ref[...] = _ffn_residual(x_ref[...], g_ref, wg_ref, wu_ref, wd_ref)


def _ffn_call(x2d, g, wg, wu, wd):
    t, d = x2d.shape
    d_ff = wg.shape[1]
    row = pl.BlockSpec((TOKEN_TILE, d), lambda i: (i, 0))
    return pl.pallas_call(
        _ffn_kernel,
        out_shape=jax.ShapeDtypeStruct((t, d), F32),
        grid=(t // TOKEN_TILE,),
        in_specs=[row, _const_spec((1, d)), _const_spec((d, d_ff)),
                  _const_spec((d, d_ff)), _const_spec((d_ff, d))],
        out_specs=row,
        compiler_params=pltpu.CompilerParams(
            dimension_semantics=("parallel",), vmem_limit_bytes=48 * MIB),
        name="ffn",
    )(x2d, g, wg, wu, wd)


def _head_rms_norm(x, gain2):
    rows, width = x.shape
    low = lax.broadcasted_iota(jnp.int32, (rows, LANES), 1) < HEAD_DIM
    out = []
    for c in range(0, width, LANES):
        xs = x[:, c:c + LANES]
        sq = xs * xs
        ms_lo = jnp.sum(jnp.where(low, sq, 0.0), axis=-1, keepdims=True) * (1.0 / HEAD_DIM)
        ms_hi = jnp.sum(jnp.where(low, 0.0, sq), axis=-1, keepdims=True) * (1.0 / HEAD_DIM)
        r = jnp.where(low, lax.rsqrt(ms_lo + EPS), lax.rsqrt(ms_hi + EPS))
        out.append(xs * r * gain2)
    return jnp.concatenate(out, axis=-1)


def _proj_kernel(d_rnn, d_att, d_model,
                 x_ref, g_ref, w_ref, gbias_ref, qg_ref, kg_ref,
                 xr_ref, yr_ref, q_ref, k_ref, v_ref, ga_ref, gb_ref):
    h = _rms_norm(x_ref[...], g_ref[...]).astype(BF16)
    c = 2 * d_rnn
    xy = _dot(h, w_ref[:, 0:c])
    xr_ref[...] = xy[:, :d_rnn]
    yr_ref[...] = xy[:, d_rnn:]
    q = _dot(h, w_ref[:, c:c + d_att])
    q_ref[...] = _head_rms_norm(q, qg_ref[...]).astype(BF16)
    k = _dot(h, w_ref[:, c + d_att:c + 2 * d_att])
    k_ref[...] = _head_rms_norm(k, kg_ref[...]).astype(BF16)
    v_ref[...] = _dot(h, w_ref[:, c + 2 * d_att:c + 3 * d_att]).astype(BF16)
    c = c + 3 * d_att
    ga = _dot(h, w_ref[:, c:c + d_model]) + gbias_ref[:, :d_model]
    ga_ref[...] = jax.nn.sigmoid(ga)
    gb = _dot(h, w_ref[:, c + d_model:c + 2 * d_model]) + gbias_ref[:, d_model:]
    gb_ref[...] = jax.nn.sigmoid(gb)


def _proj_call(x2d, g, w_in, gate_bias, q_gain2, k_gain2, d_rnn, d_att):
    t, d = x2d.shape
    d_in = w_in.shape[1]
    row = lambda w: pl.BlockSpec((TOKEN_TILE, w), lambda i: (i, 0))
    sds = lambda w, dt: jax.ShapeDtypeStruct((t, w), dt)
    return pl.pallas_call(
        functools.partial(_proj_kernel, d_rnn, d_att, d),
        out_shape=(sds(d_rnn, F32), sds(d_rnn, F32), sds(d_att, BF16), sds(d_att, BF16),
                   sds(d_att, BF16), sds(d, F32), sds(d, F32)),
        grid=(t // TOKEN_TILE,),
        in_specs=[row(d), _const_spec((1, d)), _const_spec((d, d_in)), _const_spec((1, 2 * d)),
                  _const_spec((1, LANES)), _const_spec((1, LANES))],
        out_specs=(row(d_rnn), row(d_rnn), row(d_att), row(d_att), row(d_att), row(d), row(d)),
        compiler_params=pltpu.CompilerParams(
            dimension_semantics=("parallel",), vmem_limit_bytes=56 * MIB),
        name="proj",
    )(x2d, g, w_in, gate_bias, q_gain2, k_gain2)


def _rnn_kernel(xr_ref, yr_ref, cw_ref, cb_ref, wax_ref, ba_ref, bx_ref, lam_ref,
                za_ref, xbuf_ref, h_ref):
    rows, c = xr_ref.shape[1], xr_ref.shape[2]

    @pl.when(pl.program_id(1) == 0)
    def _():
        xbuf_ref[0:SUBLANES, :] = jnp.zeros((SUBLANES, c), F32)
        h_ref[...] = jnp.zeros((SUBLANES, c), F32)

    xbuf_ref[SUBLANES:SUBLANES + rows, :] = xr_ref[0]
    u = cb_ref[...]
    for j in range(CONV_WIDTH):
        off = SUBLANES - (CONV_WIDTH - 1) + j
        u = u + xbuf_ref[off:off + rows, :] * cw_ref[j:j + 1, :]
    xbuf_ref[0:SUBLANES, :] = xbuf_ref[rows:rows + SUBLANES, :]

    gates = _dot(u.astype(BF16), wax_ref[...])
    r = jax.nn.sigmoid(gates[:, :c] + ba_ref[...])
    i = jax.nn.sigmoid(gates[:, c:] + bx_ref[...])
    log_a = (-LRU_C) * r * jax.nn.softplus(-lam_ref[...])
    a = jnp.exp(log_a)
    b = jnp.sqrt(-jnp.tanh(log_a) * (a * a + 1.0)) * (i * u)

    t_idx = lax.broadcasted_iota(jnp.int32, (rows, c), 0)
    d = 1
    while d < rows:
        keep = t_idx >= d
        a_prev = jnp.where(keep, pltpu.roll(a, d, axis=0), 1.0)
        b_prev = jnp.where(keep, pltpu.roll(b, d, axis=0), 0.0)
        b = a * b_prev + b
        a = a * a_prev
        d *= 2
    h = a * h_ref[0:1, :] + b
    h_ref[...] = jnp.broadcast_to(h[rows - 1:rows, :], (SUBLANES, c))

    za_ref[0] = (jax.nn.gelu(yr_ref[0]) * h).astype(BF16)


def _rnn_call(xr, yr, conv_w, conv_b, w_ax, b_a, b_x, lam):
    bsz, s, c = xr.shape
    blk = pl.BlockSpec((1, TIME_TILE, c), lambda b, t: (b, t, 0))
    return pl.pallas_call(
        _rnn_kernel,
        out_shape=jax.ShapeDtypeStruct((bsz, s, c), BF16),
        grid=(bsz, s // TIME_TILE),
        in_specs=[blk, blk, _const_spec((CONV_WIDTH, c)), _const_spec((1, c)),
                  _const_spec((c, 2 * c)), _const_spec((1, c)), _const_spec((1, c)),
                  _const_spec((1, c))],
        out_specs=blk,
        scratch_shapes=[pltpu.VMEM((TIME_TILE + SUBLANES, c), F32),
                        pltpu.VMEM((SUBLANES, c), F32)],
        compiler_params=pltpu.CompilerParams(
            dimension_semantics=("arbitrary", "arbitrary"), vmem_limit_bytes=48 * MIB),
        name="rnn",
    )(xr, yr, conv_w, conv_b, w_ax, b_a, b_x, lam)


def _attn_kernel(q_ref, k_ref, v_ref, bias_ref, mask_ref, o_ref):
    step = pl.program_id(2)
    scale = HEAD_DIM ** -0.5

    def unit(j, carry):
        gi = step * ATT_SUB + j
        var = jnp.minimum(gi, ATT_NVAR - 1)
        w0 = pl.multiple_of(jnp.maximum(gi * ATT_QB - LEFT_CHUNKS * CHUNK, 0), ATT_QB)
        r0 = pl.multiple_of(j * ATT_QB, ATT_QB)
        q = q_ref[0, pl.ds(r0, ATT_QB), :]
        kw = k_ref[0, pl.ds(w0, ATT_W), :]
        vw = v_ref[0, pl.ds(w0, ATT_W), :]
        valid = mask_ref[var] > 0.5
        outs = []
        for h in range(ATT_HG):
            sl = slice(h * HEAD_DIM, (h + 1) * HEAD_DIM)
            s = lax.dot_general(q[:, sl], kw[:, sl], (((1,), (1,)), ((), ())),
                                preferred_element_type=F32)
            s = jnp.where(valid, s * scale + bias_ref[var, h], -1e30)
            m = jnp.max(s, axis=-1, keepdims=True)
            e = jnp.exp(s - m)
            p = (e / jnp.sum(e, axis=-1, keepdims=True)).astype(BF16)
            outs.append(_dot(p, vw[:, sl]))
        o_ref[0, pl.ds(r0, ATT_QB), :] = jnp.concatenate(outs, axis=-1).astype(BF16)
        return carry

    lax.fori_loop(0, ATT_SUB, unit, 0)


def _attn_call(q, k, v, bias, mask):
    bsz, s, d_att = q.shape
    lanes = ATT_HG * HEAD_DIM
    rows = ATT_SUB * ATT_QB
    qo = pl.BlockSpec((1, rows, lanes), lambda b, g, i: (b, i, g))
    kv = pl.BlockSpec((1, s, lanes), lambda b, g, i: (b, 0, g))
    return pl.pallas_call(
        _attn_kernel,
        out_shape=jax.ShapeDtypeStruct((bsz, s, d_att), BF16),
        grid=(bsz, d_att // lanes, s // rows),
        in_specs=[qo, kv, kv,
                  pl.BlockSpec((ATT_NVAR, ATT_HG, ATT_QB, ATT_W), lambda b, g, i: (0, g, 0, 0)),
                  _const_spec((ATT_NVAR, ATT_QB, ATT_W))],
        out_specs=qo,
        compiler_params=pltpu.CompilerParams(
            dimension_semantics=("parallel", "parallel", "arbitrary"),
            vmem_limit_bytes=48 * MIB),
        name="attn",
    )(q, k, v, bias, mask)


def _attn_window_tables():
    var = np.arange(ATT_NVAR)[:, None, None]
    q_pos = var * ATT_QB + np.arange(ATT_QB)[None, :, None]
    k_pos = np.arange(ATT_W)[None, None, :]
    idx = np.clip(q_pos - k_pos, -REL_CLIP, REL_CLIP) + REL_CLIP
    qc, kc = q_pos // CHUNK, k_pos // CHUNK
    valid = (kc <= qc) & (kc >= qc - LEFT_CHUNKS)
    return idx.astype(np.int32), valid.astype(np.float32)


def _merge_kernel(x_ref, za_ref, at_ref, ga_ref, gb_ref, wa_ref, wb_ref, wo_ref,
                  g_ref, wg_ref, wu_ref, wd_ref, o_ref):
    y_a = _dot(za_ref[...], wa_ref[...])
    y_b = _dot(at_ref[...], wb_ref[...])
    mix = (ga_ref[...] * y_a + gb_ref[...] * y_b).astype(BF16)
    x = x_ref[...] + _dot(mix, wo_ref[...])
    o_ref[...] = _ffn_residual(x, g_ref, wg_ref, wu_ref, wd_ref)


def _merge_call(x2d, za, at, ga, gb, w_up_a, w_up_b, w_out, g, wg, wu, wd):
    t, d = x2d.shape
    d_rnn, d_att, d_ff = za.shape[1], at.shape[1], wg.shape[1]
    row = lambda w: pl.BlockSpec((TOKEN_TILE, w), lambda i: (i, 0))
    return pl.pallas_call(
        _merge_kernel,
        out_shape=jax.ShapeDtypeStruct((t, d), F32),
        grid=(t // TOKEN_TILE,),
        in_specs=[row(d), row(d_rnn), row(d_att), row(d), row(d),
                  _const_spec((d_rnn, d)), _const_spec((d_att, d)), _const_spec((d, d)),
                  _const_spec((1, d)), _const_spec((d, d_ff)), _const_spec((d, d_ff)),
                  _const_spec((d_ff, d))],
        out_specs=row(d),
        compiler_params=pltpu.CompilerParams(
            dimension_semantics=("parallel",), vmem_limit_bytes=56 * MIB),
        name="merge",
    )(x2d, za, at, ga, gb, w_up_a, w_up_b, w_out, g, wg, wu, wd)


def _block_diag(w):
    nb, n, _ = w.shape
    eye = jnp.eye(nb, dtype=w.dtype)
    return (eye[:, None, :, None] * w[:, :, None, :]).reshape(nb * n, nb * n)


def kernel(x, norm_ffn1, ffn1_w_gate, ffn1_w_up, ffn1_w_down, norm_mix, w_in, gate_bias,
           conv_w, conv_b, rg_w_a, rg_b_a, rg_w_x, rg_b_x, rg_lambda, w_up_a,
           q_gain, k_gain, rel_table, w_up_b, w_out,
           norm_ffn2, ffn2_w_gate, ffn2_w_up, ffn2_w_down):
    bsz, s, d = x.shape
    depth = w_in.shape[0]
    d_rnn = conv_w.shape[-1]
    d_att = w_up_b.shape[1]
    assert q_gain.shape[-1] == HEAD_DIM and rel_table.shape[-1] == 2 * REL_CLIP + 1
    assert (bsz * s) % TOKEN_TILE == 0 and s % TIME_TILE == 0 and s % (ATT_SUB * ATT_QB) == 0

    idx, valid = _attn_window_tables()
    mask = jnp.asarray(valid)
    row = lambda a: a.reshape(1, -1).astype(F32)
    tile2 = lambda a: jnp.tile(a.reshape(1, -1).astype(F32), (1, LANES // HEAD_DIM))

    x2d = x.reshape(bsz * s, d)
    for l in range(depth):
        x2d = _ffn_call(x2d, row(norm_ffn1[l]), ffn1_w_gate[l].astype(BF16),
                        ffn1_w_up[l].astype(BF16), ffn1_w_down[l].astype(BF16))
        xr, yr, q, k, v, ga, gb = _proj_call(
            x2d, row(norm_mix[l]), w_in[l].astype(BF16), row(gate_bias[l]),
            tile2(q_gain[l]), tile2(k_gain[l]), d_rnn, d_att)
        w_ax = jnp.concatenate([_block_diag(rg_w_a[l]), _block_diag(rg_w_x[l])],
                               axis=1).astype(BF16)
        za = _rnn_call(xr.reshape(bsz, s, d_rnn), yr.reshape(bsz, s, d_rnn),
                       conv_w[l].astype(F32), row(conv_b[l]), w_ax, row(rg_b_a[l]),
                       row(rg_b_x[l]), row(rg_lambda[l]))
        bias = jnp.transpose(rel_table[l].astype(F32)[:, idx], (1, 0, 2, 3))
        at = _attn_call(q.reshape(bsz, s, d_att), k.reshape(bsz, s, d_att),
                        v.reshape(bsz, s, d_att), bias, mask)
        x2d = _merge_call(x2d, za.reshape(bsz * s, d_rnn), at.reshape(bsz * s, d_att), ga, gb,
                          w_up_a[l].astype(BF16), w_up_b[l].astype(BF16), w_out[l].astype(BF16),
                          row(norm_ffn2[l]), ffn2_w_gate[l].astype(BF16),
                          ffn2_w_up[l].astype(BF16), ffn2_w_down[l].astype(BF16))
    return x2d.reshape(bsz, s, d)
```

```python
import functools

import numpy as np
import jax
import jax.numpy as jnp
from jax import lax
from jax.experimental import pallas as pl
from jax.experimental.pallas import tpu as pltpu

F32 = jnp.float32
BF16 = jnp.bfloat16

LANES = 128
SUBLANES = 8
MXU_DIM = 256
MIB = 1024 * 1024

CHUNK = 64
LEFT_CHUNKS = 8
HEAD_DIM = 64
REL_CLIP = 128
CONV_WIDTH = 4
LRU_C = 8.0
EPS = 1e-6

TOKEN_TILE = 512
FF_TILE = 3 * MXU_DIM
TIME_TILE = 256
ATT_QB = MXU_DIM
ATT_LEFT = LEFT_CHUNKS * CHUNK
ATT_W = ATT_QB + ATT_LEFT
ATT_SUB = 4
ATT_HG = MXU_DIM // HEAD_DIM
ATT_NVAR = ATT_LEFT // ATT_QB + 1
assert ATT_LEFT % ATT_QB == 0 and ATT_QB % CHUNK == 0 and TOKEN_TILE % ATT_QB == 0


def _const_spec(shape):
    nd = len(shape)
    return pl.BlockSpec(shape, lambda *_: (0,) * nd, pipeline_mode=pl.Buffered(1))


def _rms_norm(x, g):
    ms = jnp.mean(x * x, axis=-1, keepdims=True)
    return x * lax.rsqrt(ms + EPS) * g


def _dot(a, b):
    return jnp.dot(a, b, preferred_element_type=F32)


def _dot_nt(a, b):
    return lax.dot_general(a, b, (((1,), (1,)), ((), ())), preferred_element_type=F32)


def _ffn_residual(x, g_ref, wg_ref, wu_ref, wd_ref):
    d_ff = wg_ref.shape[1]
    h = _rms_norm(x, g_ref[...]).astype(BF16)
    acc = None
    for c0 in range(0, d_ff, FF_TILE):
        c1 = min(c0 + FF_TILE, d_ff)
        gate = _dot(h, wg_ref[:, c0:c1])
        up = _dot(h, wu_ref[:, c0:c1])
        act = (gate * jax.nn.sigmoid(gate) * up).astype(BF16)
        part = _dot(act, wd_ref[c0:c1, :])
        acc = part if acc is None else acc + part
    return x + 0.5 * acc


def _ffn_kernel(x_ref, g_ref, wg_ref, wu_ref, wd_ref, o_ref):
    o_ref[...] = _ffn_residual(x_ref[...], g_ref, wg_ref, wu_ref, wd_ref)


def _ffn_call(x2d, g, wg, wu, wd):
    t, d = x2d.shape
    d_ff = wg.shape[1]
    row = pl.BlockSpec((TOKEN_TILE, d), lambda i: (i, 0))
    return pl.pallas_call(
        _ffn_kernel,
        out_shape=jax.ShapeDtypeStruct((t, d), F32),
        grid=(t // TOKEN_TILE,),
        in_specs=[row, _const_spec((1, d)), _const_spec((d, d_ff)),
                  _const_spec((d, d_ff)), _const_spec((d_ff, d))],
        out_specs=row,
        compiler_params=pltpu.CompilerParams(
            dimension_semantics=("parallel",), vmem_limit_bytes=48 * MIB),
        name="ffn",
    )(x2d, g, wg, wu, wd)


def _head_rms_norm(x, gain2):
    rows, width = x.shape
    low = lax.broadcasted_iota(jnp.int32, (rows, LANES), 1) < HEAD_DIM
    out = []
    for c in range(0, width, LANES):
        xs = x[:, c:c + LANES]
        sq = xs * xs
        ms_lo = jnp.sum(jnp.where(low, sq, 0.0), axis=-1, keepdims=True) * (1.0 / HEAD_DIM)
        ms_hi = jnp.sum(jnp.where(low, 0.0, sq), axis=-1, keepdims=True) * (1.0 / HEAD_DIM)
        r = jnp.where(low, lax.rsqrt(ms_lo + EPS), lax.rsqrt(ms_hi + EPS))
        out.append(xs * r * gain2)
    return jnp.concatenate(out, axis=-1)


def _head_rms_norm_t(xt, gain_t):
    out = []
    for r0 in range(0, xt.shape[0], HEAD_DIM):
        xs = xt[r0:r0 + HEAD_DIM, :]
        ms = jnp.mean(xs * xs, axis=0, keepdims=True)
        out.append(xs * lax.rsqrt(ms + EPS) * gain_t)
    return jnp.concatenate(out, axis=0)


def _proj_kernel(d_rnn, d_att, d_model,
                 x_ref, g_ref, w_ref, wq_ref, wv_ref, gbias_ref, qg_ref, kg_ref,
                 xr_ref, yr_ref, k_ref, ga_ref, gb_ref, qt_ref, vt_ref):
    h = _rms_norm(x_ref[...], g_ref[...]).astype(BF16)
    c = 2 * d_rnn
    xy = _dot(h, w_ref[:, 0:c])
    xr_ref[...] = xy[:, :d_rnn]
    yr_ref[...] = xy[:, d_rnn:]
    k = _dot(h, w_ref[:, c:c + d_att])
    k_ref[...] = _head_rms_norm(k, kg_ref[...]).astype(BF16)
    c = c + d_att
    ga = _dot(h, w_ref[:, c:c + d_model]) + gbias_ref[:, :d_model]
    ga_ref[...] = jax.nn.sigmoid(ga)
    gb = _dot(h, w_ref[:, c + d_model:c + 2 * d_model]) + gbias_ref[:, d_model:]
    gb_ref[...] = jax.nn.sigmoid(gb)
    qt = _head_rms_norm_t(_dot_nt(wq_ref[...], h), qg_ref[...]).astype(BF16)
    vt = _dot_nt(wv_ref[...], h).astype(BF16)
    for n in range(qt_ref.shape[0]):
        qt_ref[n] = qt[:, n * ATT_QB:(n + 1) * ATT_QB]
        vt_ref[n] = vt[:, n * ATT_QB:(n + 1) * ATT_QB]


def _proj_call(x2d, g, w_main, w_qt, w_vt, gate_bias, q_gain_t, k_gain2, d_rnn, d_att):
    t, d = x2d.shape
    nt = TOKEN_TILE // ATT_QB
    row = lambda w: pl.BlockSpec((TOKEN_TILE, w), lambda i: (i, 0))
    sds = lambda w, dt: jax.ShapeDtypeStruct((t, w), dt)
    tiles = pl.BlockSpec((nt, d_att, ATT_QB), lambda i: (i, 0, 0))
    tiles_sds = jax.ShapeDtypeStruct((t // ATT_QB, d_att, ATT_QB), BF16)
    return pl.pallas_call(
        functools.partial(_proj_kernel, d_rnn, d_att, d),
        out_shape=(sds(d_rnn, F32), sds(d_rnn, F32), sds(d_att, BF16), sds(d, F32), sds(d, F32),
                   tiles_sds, tiles_sds),
        grid=(t // TOKEN_TILE,),
        in_specs=[row(d), _const_spec((1, d)), _const_spec(w_main.shape),
                  _const_spec(w_qt.shape), _const_spec(w_vt.shape), _const_spec((1, 2 * d)),
                  _const_spec((HEAD_DIM, TOKEN_TILE)), _const_spec((1, LANES))],
        out_specs=(row(d_rnn), row(d_rnn), row(d_att), row(d), row(d), tiles, tiles),
        compiler_params=pltpu.CompilerParams(
            dimension_semantics=("parallel",), vmem_limit_bytes=56 * MIB),
        name="proj",
    )(x2d, g, w_main, w_qt, w_vt, gate_bias, q_gain_t, k_gain2)


def _rnn_kernel(xr_ref, yr_ref, cw_ref, cb_ref, wax_ref, ba_ref, bx_ref, lam_ref,
                za_ref, xbuf_ref, h_ref, a_ref, b_ref):
    rows, c = xr_ref.shape[1], xr_ref.shape[2]
    groups = rows // SUBLANES

    @pl.when(pl.program_id(1) == 0)
    def _():
        xbuf_ref[0:SUBLANES, :] = jnp.zeros((SUBLANES, c), F32)
        h_ref[...] = jnp.zeros((1, c), F32)

    xbuf_ref[SUBLANES:SUBLANES + rows, :] = xr_ref[0]
    u = cb_ref[...]
    for j in range(CONV_WIDTH):
        off = SUBLANES - (CONV_WIDTH - 1) + j
        u = u + xbuf_ref[off:off + rows, :] * cw_ref[j:j + 1, :]
    xbuf_ref[0:SUBLANES, :] = xbuf_ref[rows:rows + SUBLANES, :]

    gates = _dot(u.astype(BF16), wax_ref[...])
    r = jax.nn.sigmoid(gates[:, :c] + ba_ref[...])
    i = jax.nn.sigmoid(gates[:, c:] + bx_ref[...])
    log_a = (-LRU_C) * r * jax.nn.softplus(-lam_ref[...])
    a = jnp.exp(log_a)
    b = jnp.sqrt(-jnp.tanh(log_a) * (a * a + 1.0)) * (i * u)

    a = a.reshape(groups, SUBLANES, c)
    b = b.reshape(groups, SUBLANES, c)
    t_idx = lax.broadcasted_iota(jnp.int32, (groups, SUBLANES, c), 1)
    d = 1
    while d < SUBLANES:
        keep = t_idx >= d
        a_prev = jnp.where(keep, pltpu.roll(a, d, axis=1), 1.0)
        b_prev = jnp.where(keep, pltpu.roll(b, d, axis=1), 0.0)
        b = a * b_prev + b
        a = a * a_prev
        d *= 2
    a_ref[...] = a.reshape(rows, c)
    b_ref[...] = b.reshape(rows, c)

    h_prev = h_ref[...]
    for g in range(groups):
        sl = slice(g * SUBLANES, (g + 1) * SUBLANES)
        h = a_ref[sl, :] * h_prev + b_ref[sl, :]
        za_ref[0, sl, :] = (jax.nn.gelu(yr_ref[0, sl, :]) * h).astype(BF16)
        h_prev = h[SUBLANES - 1:SUBLANES, :]
    h_ref[...] = h_prev


def _rnn_call(xr, yr, conv_w, conv_b, w_ax, b_a, b_x, lam):
    bsz, s, c = xr.shape
    blk = pl.BlockSpec((1, TIME_TILE, c), lambda b, t: (b, t, 0))
    return pl.pallas_call(
        _rnn_kernel,
        out_shape=jax.ShapeDtypeStruct((bsz, s, c), BF16),
        grid=(bsz, s // TIME_TILE),
        in_specs=[blk, blk, _const_spec((CONV_WIDTH, c)), _const_spec((1, c)),
                  _const_spec((c, 2 * c)), _const_spec((1, c)), _const_spec((1, c)),
                  _const_spec((1, c))],
        out_specs=blk,
        scratch_shapes=[pltpu.VMEM((TIME_TILE + SUBLANES, c), F32),
                        pltpu.VMEM((1, c), F32),
                        pltpu.VMEM((TIME_TILE, c), F32),
                        pltpu.VMEM((TIME_TILE, c), F32)],
        compiler_params=pltpu.CompilerParams(
            dimension_semantics=("arbitrary", "arbitrary"), vmem_limit_bytes=48 * MIB),
        name="rnn",
    )(xr, yr, conv_w, conv_b, w_ax, b_a, b_x, lam)


def _attn_kernel(qt_ref, k_ref, vt_ref, bias_ref, mask_ref, o_ref):
    step = pl.program_id(2)
    lanes = ATT_HG * HEAD_DIM
    nblk = ATT_W // ATT_QB

    def unit(j, carry):
        gi = step * ATT_SUB + j
        var = jnp.minimum(gi, ATT_NVAR - 1)
        wb = jnp.maximum(gi - (ATT_NVAR - 1), 0)
        w0 = pl.multiple_of(wb * ATT_QB, ATT_QB)
        kw = k_ref[0, pl.ds(w0, ATT_W), :]
        qt = qt_ref[0, j]
        valid = mask_ref[var] > 0.5
        outs = []
        for h in range(ATT_HG):
            r0, r1 = h * HEAD_DIM, (h + 1) * HEAD_DIM
            pieces = []
            if r0:
                pieces.append(jnp.zeros((r0, ATT_QB), BF16))
            pieces.append(qt[r0:r1, :])
            if r1 < lanes:
                pieces.append(jnp.zeros((lanes - r1, ATT_QB), BF16))
            s = _dot(kw, jnp.concatenate(pieces, axis=0))
            s = jnp.where(valid, s + bias_ref[var, h], -1e30)
            m = jnp.max(s, axis=0, keepdims=True)
            e = jnp.exp(s - m)
            l = jnp.sum(e, axis=0, keepdims=True)
            p = e.astype(BF16)
            o = None
            for n in range(nblk):
                part = _dot(vt_ref[0, wb + n, r0:r1, :], p[n * ATT_QB:(n + 1) * ATT_QB, :])
                o = part if o is None else o + part
            outs.append(o / l)
        ot = jnp.concatenate(outs, axis=0)
        o_ref[0, pl.ds(pl.multiple_of(j * ATT_QB, ATT_QB), ATT_QB), :] = ot.T.astype(BF16)
        return carry

    lax.fori_loop(0, ATT_SUB, unit, 0)


def _attn_call(qt, k, vt, bias, mask):
    bsz, s, d_att = k.shape
    lanes = ATT_HG * HEAD_DIM
    rows = ATT_SUB * ATT_QB
    nt = s // ATT_QB
    return pl.pallas_call(
        _attn_kernel,
        out_shape=jax.ShapeDtypeStruct((bsz, s, d_att), BF16),
        grid=(bsz, d_att // lanes, s // rows),
        in_specs=[pl.BlockSpec((1, ATT_SUB, lanes, ATT_QB), lambda b, g, i: (b, i, g, 0)),
                  pl.BlockSpec((1, s, lanes), lambda b, g, i: (b, 0, g)),
                  pl.BlockSpec((1, nt, lanes, ATT_QB), lambda b, g, i: (b, 0, g, 0)),
                  pl.BlockSpec((ATT_NVAR, ATT_HG, ATT_W, ATT_QB), lambda b, g, i: (0, g, 0, 0)),
                  _const_spec((ATT_NVAR, ATT_W, ATT_QB))],
        out_specs=pl.BlockSpec((1, rows, lanes), lambda b, g, i: (b, i, g)),
        compiler_params=pltpu.CompilerParams(
            dimension_semantics=("parallel", "parallel", "arbitrary"),
            vmem_limit_bytes=56 * MIB),
        name="attn",
    )(qt, k, vt, bias, mask)


def _attn_valid_mask():
    var = np.arange(ATT_NVAR)[:, None, None]
    k_pos = np.arange(ATT_W)[None, :, None]
    q_pos = var * ATT_QB + np.arange(ATT_QB)[None, None, :]
    qc, kc = q_pos // CHUNK, k_pos // CHUNK
    return ((kc <= qc) & (kc >= qc - LEFT_CHUNKS)).astype(np.float32)


def _attn_bias(rel_table):
    nh = rel_table.shape[0]
    t = rel_table.astype(F32)
    pad = ATT_W - 1 - REL_CLIP
    strip = jnp.concatenate([jnp.broadcast_to(t[:, :1], (nh, pad)), t,
                             jnp.broadcast_to(t[:, -1:], (nh, pad + 1))], axis=1)
    p = strip.shape[1]
    flat = jnp.tile(strip, (1, ATT_W))[:, :ATT_W * (p - 1)]
    full = flat.reshape(nh, ATT_W, p - 1)[:, :, ATT_W - 1:2 * ATT_W - 1]
    return jnp.stack([full[:, :, v * ATT_QB:(v + 1) * ATT_QB] for v in range(ATT_NVAR)], axis=0)


def _merge_kernel(x_ref, za_ref, at_ref, ga_ref, gb_ref, wa_ref, wb_ref, wo_ref,
                  g_ref, wg_ref, wu_ref, wd_ref, o_ref):
    y_a = _dot(za_ref[...], wa_ref[...])
    y_b = _dot(at_ref[...], wb_ref[...])
    mix = (ga_ref[...] * y_a + gb_ref[...] * y_b).astype(BF16)
    x = x_ref[...] + _dot(mix, wo_ref[...])
    o_ref[...] = _ffn_residual(x, g_ref, wg_ref, wu_ref, wd_ref)


def _merge_call(x2d, za, at, ga, gb, w_up_a, w_up_b, w_out, g, wg, wu, wd):
    t, d = x2d.shape
    d_rnn, d_att, d_ff = za.shape[1], at.shape[1], wg.shape[1]
    row = lambda w: pl.BlockSpec((TOKEN_TILE, w), lambda i: (i, 0))
    return pl.pallas_call(
        _merge_kernel,
        out_shape=jax.ShapeDtypeStruct((t, d), F32),
        grid=(t // TOKEN_TILE,),
        in_specs=[row(d), row(d_rnn), row(d_att), row(d), row(d),
                  _const_spec((d_rnn, d)), _const_spec((d_att, d)), _const_spec((d, d)),
                  _const_spec((1, d)), _const_spec((d, d_ff)), _const_spec((d, d_ff)),
                  _const_spec((d_ff, d))],
        out_specs=row(d),
        compiler_params=pltpu.CompilerParams(
            dimension_semantics=("parallel",), vmem_limit_bytes=56 * MIB),
        name="merge",
    )(x2d, za, at, ga, gb, w_up_a, w_up_b, w_out, g, wg, wu, wd)


def _block_diag(w):
    nb, n, _ = w.shape
    eye = jnp.eye(nb, dtype=w.dtype)
    return (eye[:, None, :, None] * w[:, :, None, :]).reshape(nb * n, nb * n)


def kernel(x, norm_ffn1, ffn1_w_gate, ffn1_w_up, ffn1_w_down, norm_mix, w_in, gate_bias,
           conv_w, conv_b, rg_w_a, rg_b_a, rg_w_x, rg_b_x, rg_lambda, w_up_a,
           q_gain, k_gain, rel_table, w_up_b, w_out,
           norm_ffn2, ffn2_w_gate, ffn2_w_up, ffn2_w_down):
    bsz, s, d = x.shape
    depth = w_in.shape[0]
    d_rnn = conv_w.shape[-1]
    d_att = w_up_b.shape[1]
    assert q_gain.shape[-1] == HEAD_DIM and rel_table.shape[-1] == 2 * REL_CLIP + 1
    assert (bsz * s) % TOKEN_TILE == 0 and s % TIME_TILE == 0 and s % (ATT_SUB * ATT_QB) == 0
    assert s % TOKEN_TILE == 0 and d_att == d

    mask = jnp.asarray(_attn_valid_mask())
    row = lambda a: a.reshape(1, -1).astype(F32)
    c_q, c_k, c_v, c_g = 2 * d_rnn, 2 * d_rnn + d_att, 2 * d_rnn + 2 * d_att, 2 * d_rnn + 3 * d_att
    q_scale = HEAD_DIM ** -0.5

    x2d = x.reshape(bsz * s, d)
    for l in range(depth):
        x2d = _ffn_call(x2d, row(norm_ffn1[l]), ffn1_w_gate[l].astype(BF16),
                        ffn1_w_up[l].astype(BF16), ffn1_w_down[l].astype(BF16))
        w = w_in[l]
        w_main = jnp.concatenate([w[:, :c_q], w[:, c_k:c_v], w[:, c_g:]], axis=1).astype(BF16)
        w_qt = w[:, c_q:c_k].T.astype(BF16)
        w_vt = w[:, c_v:c_g].T.astype(BF16)
        q_gain_t = jnp.broadcast_to((q_gain[l].astype(F32) * q_scale)[:, None],
                                    (HEAD_DIM, TOKEN_TILE))
        k_gain2 = jnp.tile(row(k_gain[l]), (1, LANES // HEAD_DIM))
        xr, yr, k, ga, gb, qt, vt = _proj_call(
            x2d, row(norm_mix[l]), w_main, w_qt, w_vt, row(gate_bias[l]), q_gain_t, k_gain2,
            d_rnn, d_att)
        w_ax = jnp.concatenate([_block_diag(rg_w_a[l]), _block_diag(rg_w_x[l])],
                               axis=1).astype(BF16)
        za = _rnn_call(xr.reshape(bsz, s, d_rnn), yr.reshape(bsz, s, d_rnn),
                       conv_w[l].astype(F32), row(conv_b[l]), w_ax, row(rg_b_a[l]),
                       row(rg_b_x[l]), row(rg_lambda[l]))
        at = _attn_call(qt.reshape(bsz, s // ATT_QB, d_att, ATT_QB), k.reshape(bsz, s, d_att),
                        vt.reshape(bsz, s // ATT_QB, d_att, ATT_QB),
                        _attn_bias(rel_table[l]), mask)
        x2d = _merge_call(x2d, za.reshape(bsz * s, d_rnn), at.reshape(bsz * s, d_att), ga, gb,
                          w_up_a[l].astype(BF16), w_up_b[l].astype(BF16), w_out[l].astype(BF16),
                          row(norm_ffn2[l]), ffn2_w_gate[l].astype(BF16),
                          ffn2_w_up[l].astype(BF16), ffn2_w_down[l].astype(BF16))
    return x2d.reshape(bsz, s, d)
```

```python
import functools

import numpy as np
import jax
import jax.numpy as jnp
from jax import lax
from jax.experimental import pallas as pl
from jax.experimental.pallas import tpu as pltpu

F32 = jnp.float32
BF16 = jnp.bfloat16

LANES = 128
SUBLANES = 8
MXU_DIM = 256
MIB = 1024 * 1024

CHUNK = 64
LEFT_CHUNKS = 8
HEAD_DIM = 64
REL_CLIP = 128
CONV_WIDTH = 4
LRU_C = 8.0
EPS = 1e-6

TOKEN_TILE = 512
FF_TILE = 3 * MXU_DIM
TIME_TILE = 256
ATT_QB = MXU_DIM
ATT_NBLK = LEFT_CHUNKS * CHUNK // ATT_QB + 1
ATT_SUB = 4
ATT_HG = MXU_DIM // HEAD_DIM
assert (LEFT_CHUNKS * CHUNK) % ATT_QB == 0 and ATT_QB % CHUNK == 0 and TOKEN_TILE % ATT_QB == 0


def _const_spec(shape):
    nd = len(shape)
    return pl.BlockSpec(shape, lambda *_: (0,) * nd, pipeline_mode=pl.Buffered(1))


def _rms_norm(x, g):
    ms = jnp.mean(x * x, axis=-1, keepdims=True)
    return x * lax.rsqrt(ms + EPS) * g


def _dot(a, b):
    return jnp.dot(a, b, preferred_element_type=F32)


def _dot_nt(a, b):
    return lax.dot_general(a, b, (((1,), (1,)), ((), ())), preferred_element_type=F32)


def _ffn_residual(x, g_ref, wg_ref, wu_ref, wd_ref):
    d_ff = wg_ref.shape[1]
    h = _rms_norm(x, g_ref[...]).astype(BF16)
    acc = None
    for c0 in range(0, d_ff, FF_TILE):
        c1 = min(c0 + FF_TILE, d_ff)
        gate = _dot(h, wg_ref[:, c0:c1])
        up = _dot(h, wu_ref[:, c0:c1])
        act = (gate * jax.nn.sigmoid(gate) * up).astype(BF16)
        part = _dot(act, wd_ref[c0:c1, :])
        acc = part if acc is None else acc + part
    return x + 0.5 * acc


def _ffn_kernel(x_ref, g_ref, wg_ref, wu_ref, wd_ref, o_ref):
    o_ref[...] = _ffn_residual(x_ref[...], g_ref, wg_ref, wu_ref, wd_ref)


def _ffn_call(x2d, g, wg, wu, wd):
    t, d = x2d.shape
    d_ff = wg.shape[1]
    row = pl.BlockSpec((TOKEN_TILE, d), lambda i: (i, 0))
    return pl.pallas_call(
        _ffn_kernel,
        out_shape=jax.ShapeDtypeStruct((t, d), F32),
        grid=(t // TOKEN_TILE,),
        in_specs=[row, _const_spec((1, d)), _const_spec((d, d_ff)),
                  _const_spec((d, d_ff)), _const_spec((d_ff, d))],
        out_specs=row,
        compiler_params=pltpu.CompilerParams(
            dimension_semantics=("parallel",), vmem_limit_bytes=48 * MIB),
        name="ffn",
    )(x2d, g, wg, wu, wd)


def _head_rms_norm(x, gain2):
    rows, width = x.shape
    low = lax.broadcasted_iota(jnp.int32, (rows, LANES), 1) < HEAD_DIM
    out = []
    for c in range(0, width, LANES):
        xs = x[:, c:c + LANES]
        sq = xs * xs
        ms_lo = jnp.sum(jnp.where(low, sq, 0.0), axis=-1, keepdims=True) * (1.0 / HEAD_DIM)
        ms_hi = jnp.sum(jnp.where(low, 0.0, sq), axis=-1, keepdims=True) * (1.0 / HEAD_DIM)
        r = jnp.where(low, lax.rsqrt(ms_lo + EPS), lax.rsqrt(ms_hi + EPS))
        out.append(xs * r * gain2)
    return jnp.concatenate(out, axis=-1)


def _head_rms_norm_t(xt, gain_t):
    out = []
    for r0 in range(0, xt.shape[0], HEAD_DIM):
        xs = xt[r0:r0 + HEAD_DIM, :]
        ms = jnp.mean(xs * xs, axis=0, keepdims=True)
        out.append(xs * lax.rsqrt(ms + EPS) * gain_t)
    return jnp.concatenate(out, axis=0)


def _proj_kernel(d_rnn, d_att, d_model,
                 x_ref, g_ref, w_ref, wq_ref, wv_ref, gbias_ref, qg_ref, kg_ref,
                 xr_ref, yr_ref, k_ref, ga_ref, gb_ref, qt_ref, vt_ref):
    h = _rms_norm(x_ref[...], g_ref[...]).astype(BF16)
    c = 2 * d_rnn
    xy = _dot(h, w_ref[:, 0:c])
    xr_ref[...] = xy[:, :d_rnn]
    yr_ref[...] = xy[:, d_rnn:]
    k = _dot(h, w_ref[:, c:c + d_att])
    k_ref[...] = _head_rms_norm(k, kg_ref[...]).astype(BF16)
    c = c + d_att
    ga = _dot(h, w_ref[:, c:c + d_model]) + gbias_ref[:, :d_model]
    ga_ref[...] = jax.nn.sigmoid(ga)
    gb = _dot(h, w_ref[:, c + d_model:c + 2 * d_model]) + gbias_ref[:, d_model:]
    gb_ref[...] = jax.nn.sigmoid(gb)
    qt = _head_rms_norm_t(_dot_nt(wq_ref[...], h), qg_ref[...]).astype(BF16)
    vt = _dot_nt(wv_ref[...], h).astype(BF16)
    for n in range(qt_ref.shape[0]):
        qt_ref[n] = qt[:, n * ATT_QB:(n + 1) * ATT_QB]
        vt_ref[n] = vt[:, n * ATT_QB:(n + 1) * ATT_QB]


def _proj_call(x2d, g, w_main, w_qt, w_vt, gate_bias, q_gain_t, k_gain2, d_rnn, d_att):
    t, d = x2d.shape
    nt = TOKEN_TILE // ATT_QB
    row = lambda w: pl.BlockSpec((TOKEN_TILE, w), lambda i: (i, 0))
    sds = lambda w, dt: jax.ShapeDtypeStruct((t, w), dt)
    tiles = pl.BlockSpec((nt, d_att, ATT_QB), lambda i: (i, 0, 0))
    tiles_sds = jax.ShapeDtypeStruct((t // ATT_QB, d_att, ATT_QB), BF16)
    return pl.pallas_call(
        functools.partial(_proj_kernel, d_rnn, d_att, d),
        out_shape=(sds(d_rnn, F32), sds(d_rnn, F32), sds(d_att, BF16), sds(d, F32), sds(d, F32),
                   tiles_sds, tiles_sds),
        grid=(t // TOKEN_TILE,),
        in_specs=[row(d), _const_spec((1, d)), _const_spec(w_main.shape),
                  _const_spec(w_qt.shape), _const_spec(w_vt.shape), _const_spec((1, 2 * d)),
                  _const_spec((HEAD_DIM, TOKEN_TILE)), _const_spec((1, LANES))],
        out_specs=(row(d_rnn), row(d_rnn), row(d_att), row(d), row(d), tiles, tiles),
        compiler_params=pltpu.CompilerParams(
            dimension_semantics=("parallel",), vmem_limit_bytes=56 * MIB),
        name="proj",
    )(x2d, g, w_main, w_qt, w_vt, gate_bias, q_gain_t, k_gain2)


def _rnn_kernel(xr_ref, yr_ref, cw_ref, cb_ref, wax_ref, ba_ref, bx_ref, lam_ref,
                za_ref, xbuf_ref, h_ref, a_ref, b_ref):
    rows, c = xr_ref.shape[1], xr_ref.shape[2]
    groups = rows // SUBLANES

    @pl.when(pl.program_id(1) == 0)
    def _():
        xbuf_ref[0:SUBLANES, :] = jnp.zeros((SUBLANES, c), F32)
        h_ref[...] = jnp.zeros((1, c), F32)

    xbuf_ref[SUBLANES:SUBLANES + rows, :] = xr_ref[0]
    u = cb_ref[...]
    for j in range(CONV_WIDTH):
        off = SUBLANES - (CONV_WIDTH - 1) + j
        u = u + xbuf_ref[off:off + rows, :] * cw_ref[j:j + 1, :]
    xbuf_ref[0:SUBLANES, :] = xbuf_ref[rows:rows + SUBLANES, :]

    gates = _dot(u.astype(BF16), wax_ref[...])
    r = jax.nn.sigmoid(gates[:, :c] + ba_ref[...])
    i = jax.nn.sigmoid(gates[:, c:] + bx_ref[...])
    log_a = (-LRU_C) * r * jax.nn.softplus(-lam_ref[...])
    a = jnp.exp(log_a)
    b = jnp.sqrt(-jnp.tanh(log_a) * (a * a + 1.0)) * (i * u)

    a = a.reshape(groups, SUBLANES, c)
    b = b.reshape(groups, SUBLANES, c)
    t_idx = lax.broadcasted_iota(jnp.int32, (groups, SUBLANES, c), 1)
    d = 1
    while d < SUBLANES:
        keep = t_idx >= d
        a_prev = jnp.where(keep, pltpu.roll(a, d, axis=1), 1.0)
        b_prev = jnp.where(keep, pltpu.roll(b, d, axis=1), 0.0)
        b = a * b_prev + b
        a = a * a_prev
        d *= 2
    a_ref[...] = a.reshape(rows, c)
    b_ref[...] = b.reshape(rows, c)

    h_prev = h_ref[...]
    for g in range(groups):
        sl = slice(g * SUBLANES, (g + 1) * SUBLANES)
        h = a_ref[sl, :] * h_prev + b_ref[sl, :]
        za_ref[0, sl, :] = (jax.nn.gelu(yr_ref[0, sl, :]) * h).astype(BF16)
        h_prev = h[SUBLANES - 1:SUBLANES, :]
    h_ref[...] = h_prev


def _rnn_call(xr, yr, conv_w, conv_b, w_ax, b_a, b_x, lam):
    bsz, s, c = xr.shape
    blk = pl.BlockSpec((1, TIME_TILE, c), lambda b, t: (b, t, 0))
    return pl.pallas_call(
        _rnn_kernel,
        out_shape=jax.ShapeDtypeStruct((bsz, s, c), BF16),
        grid=(bsz, s // TIME_TILE),
        in_specs=[blk, blk, _const_spec((CONV_WIDTH, c)), _const_spec((1, c)),
                  _const_spec((c, 2 * c)), _const_spec((1, c)), _const_spec((1, c)),
                  _const_spec((1, c))],
        out_specs=blk,
        scratch_shapes=[pltpu.VMEM((TIME_TILE + SUBLANES, c), F32),
                        pltpu.VMEM((1, c), F32),
                        pltpu.VMEM((TIME_TILE, c), F32),
                        pltpu.VMEM((TIME_TILE, c), F32)],
        compiler_params=pltpu.CompilerParams(
            dimension_semantics=("arbitrary", "arbitrary"), vmem_limit_bytes=48 * MIB),
        name="rnn",
    )(xr, yr, conv_w, conv_b, w_ax, b_a, b_x, lam)


def _bias_kernel(base_ref, o_ref):
    row = lax.broadcasted_iota(jnp.int32, (ATT_QB, 2 * ATT_QB), 0)
    for n in range(ATT_NBLK):
        x = jnp.broadcast_to(base_ref[0, n], (ATT_QB, 2 * ATT_QB))
        bit = 1
        while bit < ATT_QB:
            x = jnp.where((row & bit) != 0, pltpu.roll(x, bit, axis=1), x)
            bit *= 2
        o_ref[0, n] = x[:, :ATT_QB]


def _bias_call(rel_table):
    nh = rel_table.shape[0]
    j = np.arange(2 * ATT_QB)
    rel = np.where(j < ATT_QB, j, j - 2 * ATT_QB)[None, :] + ATT_QB * np.arange(ATT_NBLK)[:, None]
    idx = np.clip(rel, -REL_CLIP, REL_CLIP) + REL_CLIP
    base = rel_table.astype(F32)[:, idx].reshape(nh, ATT_NBLK, 1, 2 * ATT_QB)
    return pl.pallas_call(
        _bias_kernel,
        out_shape=jax.ShapeDtypeStruct((nh, ATT_NBLK, ATT_QB, ATT_QB), F32),
        grid=(nh,),
        in_specs=[pl.BlockSpec((1, ATT_NBLK, 1, 2 * ATT_QB), lambda h: (h, 0, 0, 0))],
        out_specs=pl.BlockSpec((1, ATT_NBLK, ATT_QB, ATT_QB), lambda h: (h, 0, 0, 0)),
        compiler_params=pltpu.CompilerParams(dimension_semantics=("parallel",)),
        name="bias",
    )(base)


def _attn_kernel(qt_ref, k_ref, vt_ref, bias_ref, mask_ref, o_ref):
    step = pl.program_id(2)
    lanes = ATT_HG * HEAD_DIM

    def unit(j, carry):
        gi = step * ATT_SUB + j
        qt = qt_ref[0, j]
        tiles = [jnp.maximum(gi - n, 0) for n in range(ATT_NBLK)]
        kb = [k_ref[0, pl.ds(pl.multiple_of(t * ATT_QB, ATT_QB), ATT_QB), :] for t in tiles]
        valid = [mask_ref[n] > jnp.where(gi >= n, 0.5, 2.0) for n in range(ATT_NBLK)]
        outs = []
        for h in range(ATT_HG):
            r0, r1 = h * HEAD_DIM, (h + 1) * HEAD_DIM
            pieces = []
            if r0:
                pieces.append(jnp.zeros((r0, ATT_QB), BF16))
            pieces.append(qt[r0:r1, :])
            if r1 < lanes:
                pieces.append(jnp.zeros((lanes - r1, ATT_QB), BF16))
            qm = jnp.concatenate(pieces, axis=0)
            s = [jnp.where(valid[n], _dot(kb[n], qm) + bias_ref[h, n], -1e30)
                 for n in range(ATT_NBLK)]
            m = functools.reduce(jnp.maximum, [jnp.max(x, axis=0, keepdims=True) for x in s])
            e = [jnp.exp(x - m) for x in s]
            l = functools.reduce(jnp.add, [jnp.sum(x, axis=0, keepdims=True) for x in e])
            o = functools.reduce(jnp.add, [
                _dot(vt_ref[0, tiles[n], r0:r1, :], e[n].astype(BF16)) for n in range(ATT_NBLK)])
            outs.append(o / l)
        ot = jnp.concatenate(outs, axis=0)
        o_ref[0, pl.ds(pl.multiple_of(j * ATT_QB, ATT_QB), ATT_QB), :] = ot.T.astype(BF16)
        return carry

    lax.fori_loop(0, ATT_SUB, unit, 0)


def _attn_call(qt, k, vt, bias, mask):
    bsz, s, d_att = k.shape
    lanes = ATT_HG * HEAD_DIM
    rows = ATT_SUB * ATT_QB
    nt = s // ATT_QB
    return pl.pallas_call(
        _attn_kernel,
        out_shape=jax.ShapeDtypeStruct((bsz, s, d_att), BF16),
        grid=(bsz, d_att // lanes, s // rows),
        in_specs=[pl.BlockSpec((1, ATT_SUB, lanes, ATT_QB), lambda b, g, i: (b, i, g, 0)),
                  pl.BlockSpec((1, s, lanes), lambda b, g, i: (b, 0, g)),
                  pl.BlockSpec((1, nt, lanes, ATT_QB), lambda b, g, i: (b, 0, g, 0)),
                  pl.BlockSpec((ATT_HG, ATT_NBLK, ATT_QB, ATT_QB), lambda b, g, i: (g, 0, 0, 0)),
                  _const_spec((ATT_NBLK, ATT_QB, ATT_QB))],
        out_specs=pl.BlockSpec((1, rows, lanes), lambda b, g, i: (b, i, g)),
        compiler_params=pltpu.CompilerParams(
            dimension_semantics=("parallel", "parallel", "arbitrary"),
            vmem_limit_bytes=48 * MIB),
        name="attn",
    )(qt, k, vt, bias, mask)


def _attn_valid_mask():
    per = ATT_QB // CHUNK
    n = np.arange(ATT_NBLK)[:, None, None]
    kc = (np.arange(ATT_QB) // CHUNK)[None, :, None]
    qc = (np.arange(ATT_QB) // CHUNK)[None, None, :]
    dist = per * n + qc - kc
    return ((dist >= 0) & (dist <= LEFT_CHUNKS)).astype(np.float32)


def _merge_kernel(x_ref, za_ref, at_ref, ga_ref, gb_ref, wa_ref, wb_ref, wo_ref,
                  g_ref, wg_ref, wu_ref, wd_ref, o_ref):
    y_a = _dot(za_ref[...], wa_ref[...])
    y_b = _dot(at_ref[...], wb_ref[...])
    mix = (ga_ref[...] * y_a + gb_ref[...] * y_b).astype(BF16)
    x = x_ref[...] + _dot(mix, wo_ref[...])
    o_ref[...] = _ffn_residual(x, g_ref, wg_ref, wu_ref, wd_ref)


def _merge_call(x2d, za, at, ga, gb, w_up_a, w_up_b, w_out, g, wg, wu, wd):
    t, d = x2d.shape
    d_rnn, d_att, d_ff = za.shape[1], at.shape[1], wg.shape[1]
    row = lambda w: pl.BlockSpec((TOKEN_TILE, w), lambda i: (i, 0))
    return pl.pallas_call(
        _merge_kernel,
        out_shape=jax.ShapeDtypeStruct((t, d), F32),
        grid=(t // TOKEN_TILE,),
        in_specs=[row(d), row(d_rnn), row(d_att), row(d), row(d),
                  _const_spec((d_rnn, d)), _const_spec((d_att, d)), _const_spec((d, d)),
                  _const_spec((1, d)), _const_spec((d, d_ff)), _const_spec((d, d_ff)),
                  _const_spec((d_ff, d))],
        out_specs=row(d),
        compiler_params=pltpu.CompilerParams(
            dimension_semantics=("parallel",), vmem_limit_bytes=56 * MIB),
        name="merge",
    )(x2d, za, at, ga, gb, w_up_a, w_up_b, w_out, g, wg, wu, wd)


def _block_diag(w):
    nb, n, _ = w.shape
    eye = jnp.eye(nb, dtype=w.dtype)
    return (eye[:, None, :, None] * w[:, :, None, :]).reshape(nb * n, nb * n)


def kernel(x, norm_ffn1, ffn1_w_gate, ffn1_w_up, ffn1_w_down, norm_mix, w_in, gate_bias,
           conv_w, conv_b, rg_w_a, rg_b_a, rg_w_x, rg_b_x, rg_lambda, w_up_a,
           q_gain, k_gain, rel_table, w_up_b, w_out,
           norm_ffn2, ffn2_w_gate, ffn2_w_up, ffn2_w_down):
    bsz, s, d = x.shape
    depth = w_in.shape[0]
    d_rnn = conv_w.shape[-1]
    d_att = w_up_b.shape[1]
    assert q_gain.shape[-1] == HEAD_DIM and rel_table.shape[-1] == 2 * REL_CLIP + 1
    assert (bsz * s) % TOKEN_TILE == 0 and s % TIME_TILE == 0 and s % (ATT_SUB * ATT_QB) == 0
    assert s % TOKEN_TILE == 0 and d_att == d

    mask = jnp.asarray(_attn_valid_mask())
    row = lambda a: a.reshape(1, -1).astype(F32)
    c_q, c_k, c_v, c_g = 2 * d_rnn, 2 * d_rnn + d_att, 2 * d_rnn + 2 * d_att, 2 * d_rnn + 3 * d_att
    q_scale = HEAD_DIM ** -0.5

    x2d = x.reshape(bsz * s, d)
    for l in range(depth):
        x2d = _ffn_call(x2d, row(norm_ffn1[l]), ffn1_w_gate[l].astype(BF16),
                        ffn1_w_up[l].astype(BF16), ffn1_w_down[l].astype(BF16))
        w = w_in[l]
        w_main = jnp.concatenate([w[:, :c_q], w[:, c_k:c_v], w[:, c_g:]], axis=1).astype(BF16)
        w_qt = w[:, c_q:c_k].T.astype(BF16)
        w_vt = w[:, c_v:c_g].T.astype(BF16)
        q_gain_t = jnp.broadcast_to((q_gain[l].astype(F32) * q_scale)[:, None],
                                    (HEAD_DIM, TOKEN_TILE))
        k_gain2 = jnp.tile(row(k_gain[l]), (1, LANES // HEAD_DIM))
        xr, yr, k, ga, gb, qt, vt = _proj_call(
            x2d, row(norm_mix[l]), w_main, w_qt, w_vt, row(gate_bias[l]), q_gain_t, k_gain2,
            d_rnn, d_att)
        w_ax = jnp.concatenate([_block_diag(rg_w_a[l]), _block_diag(rg_w_x[l])],
                               axis=1).astype(BF16)
        za = _rnn_call(xr.reshape(bsz, s, d_rnn), yr.reshape(bsz, s, d_rnn),
                       conv_w[l].astype(F32), row(conv_b[l]), w_ax, row(rg_b_a[l]),
                       row(rg_b_x[l]), row(rg_lambda[l]))
        at = _attn_call(qt.reshape(bsz, s // ATT_QB, d_att, ATT_QB), k.reshape(bsz, s, d_att),
                        vt.reshape(bsz, s // ATT_QB, d_att, ATT_QB),
                        _bias_call(rel_table[l]), mask)
        x2d = _merge_call(x2d, za.reshape(bsz * s, d_rnn), at.reshape(bsz * s, d_att), ga, gb,
                          w_up_a[l].astype(BF16), w_up_b[l].astype(BF16), w_out[l].astype(BF16),
                          row(norm_ffn2[l]), ffn2_w_gate[l].astype(BF16),
                          ffn2_w_up[l].astype(BF16), ffn2_w_down[l].astype(BF16))
    return x2d.reshape(bsz, s, d)
```

```python
import functools

import numpy as np
import jax
import jax.numpy as jnp
from jax import lax
from jax.experimental import pallas as pl
from jax.experimental.pallas import tpu as pltpu

F32 = jnp.float32
BF16 = jnp.bfloat16

LANES = 128
SUBLANES = 8
MXU_DIM = 256
MIB = 1024 * 1024

CHUNK = 64
LEFT_CHUNKS = 8
HEAD_DIM = 64
REL_CLIP = 128
CONV_WIDTH = 4
LRU_C = 8.0
EPS = 1e-6
LOG2E = 1.4426950408889634

TOKEN_TILE = 512
FF_TILE = 3 * MXU_DIM
TIME_TILE = 256
ATT_QB = MXU_DIM
ATT_NBLK = LEFT_CHUNKS * CHUNK // ATT_QB + 1
ATT_SUB = 4
ATT_HG = MXU_DIM // HEAD_DIM
assert (LEFT_CHUNKS * CHUNK) % ATT_QB == 0 and ATT_QB % CHUNK == 0 and TOKEN_TILE % ATT_QB == 0


def _const_spec(shape):
    nd = len(shape)
    return pl.BlockSpec(shape, lambda *_: (0,) * nd, pipeline_mode=pl.Buffered(1))


def _rms_norm(x, g):
    ms = jnp.mean(x * x, axis=-1, keepdims=True)
    return x * lax.rsqrt(ms + EPS) * g


def _dot(a, b):
    return jnp.dot(a, b, preferred_element_type=F32)


def _dot_nt(a, b):
    return lax.dot_general(a, b, (((1,), (1,)), ((), ())), preferred_element_type=F32)


def _ffn_residual(x, g_ref, wg_ref, wu_ref, wd_ref):
    d_ff = wg_ref.shape[1]
    h = _rms_norm(x, g_ref[...]).astype(BF16)
    acc = None
    for c0 in range(0, d_ff, FF_TILE):
        c1 = min(c0 + FF_TILE, d_ff)
        gate = _dot(h, wg_ref[:, c0:c1])
        up = _dot(h, wu_ref[:, c0:c1])
        act = (gate * jax.nn.sigmoid(gate) * up).astype(BF16)
        part = _dot(act, wd_ref[c0:c1, :])
        acc = part if acc is None else acc + part
    return x + 0.5 * acc


def _ffn_kernel(x_ref, g_ref, wg_ref, wu_ref, wd_ref, o_ref):
    o_ref[...] = _ffn_residual(x_ref[...], g_ref, wg_ref, wu_ref, wd_ref)


def _ffn_call(x2d, g, wg, wu, wd):
    t, d = x2d.shape
    d_ff = wg.shape[1]
    row = pl.BlockSpec((TOKEN_TILE, d), lambda i: (i, 0))
    return pl.pallas_call(
        _ffn_kernel,
        out_shape=jax.ShapeDtypeStruct((t, d), F32),
        grid=(t // TOKEN_TILE,),
        in_specs=[row, _const_spec((1, d)), _const_spec((d, d_ff)),
                  _const_spec((d, d_ff)), _const_spec((d_ff, d))],
        out_specs=row,
        compiler_params=pltpu.CompilerParams(
            dimension_semantics=("parallel",), vmem_limit_bytes=48 * MIB),
        name="ffn",
    )(x2d, g, wg, wu, wd)


def _head_rms_norm(x, gain2):
    rows, width = x.shape
    low = lax.broadcasted_iota(jnp.int32, (rows, LANES), 1) < HEAD_DIM
    out = []
    for c in range(0, width, LANES):
        xs = x[:, c:c + LANES]
        sq = xs * xs
        ms_lo = jnp.sum(jnp.where(low, sq, 0.0), axis=-1, keepdims=True) * (1.0 / HEAD_DIM)
        ms_hi = jnp.sum(jnp.where(low, 0.0, sq), axis=-1, keepdims=True) * (1.0 / HEAD_DIM)
        r = jnp.where(low, lax.rsqrt(ms_lo + EPS), lax.rsqrt(ms_hi + EPS))
        out.append(xs * r * gain2)
    return jnp.concatenate(out, axis=-1)


def _head_rms_norm_t(xt, gain_t):
    out = []
    for r0 in range(0, xt.shape[0], HEAD_DIM):
        xs = xt[r0:r0 + HEAD_DIM, :]
        ms = jnp.mean(xs * xs, axis=0, keepdims=True)
        out.append(xs * lax.rsqrt(ms + EPS) * gain_t)
    return jnp.concatenate(out, axis=0)


def _proj_kernel(d_rnn, d_att, d_model,
                 x_ref, g_ref, w_ref, wq_ref, wv_ref, gbias_ref, qg_ref, kg_ref,
                 xr_ref, yr_ref, k_ref, ga_ref, gb_ref, qt_ref, vt_ref):
    h = _rms_norm(x_ref[...], g_ref[...]).astype(BF16)
    c = 2 * d_rnn
    xy = _dot(h, w_ref[:, 0:c])
    xr_ref[...] = xy[:, :d_rnn]
    yr_ref[...] = xy[:, d_rnn:]
    k = _dot(h, w_ref[:, c:c + d_att])
    k_ref[...] = _head_rms_norm(k, kg_ref[...]).astype(BF16)
    c = c + d_att
    ga = _dot(h, w_ref[:, c:c + d_model]) + gbias_ref[:, :d_model]
    ga_ref[...] = jax.nn.sigmoid(ga)
    gb = _dot(h, w_ref[:, c + d_model:c + 2 * d_model]) + gbias_ref[:, d_model:]
    gb_ref[...] = jax.nn.sigmoid(gb)
    qt = _head_rms_norm_t(_dot_nt(wq_ref[...], h), qg_ref[...]).astype(BF16)
    vt = _dot_nt(wv_ref[...], h).astype(BF16)
    for n in range(qt_ref.shape[0]):
        qt_ref[n] = qt[:, n * ATT_QB:(n + 1) * ATT_QB]
        vt_ref[n] = vt[:, n * ATT_QB:(n + 1) * ATT_QB]


def _proj_call(x2d, g, w_main, w_qt, w_vt, gate_bias, q_gain_t, k_gain2, d_rnn, d_att):
    t, d = x2d.shape
    nt = TOKEN_TILE // ATT_QB
    row = lambda w: pl.BlockSpec((TOKEN_TILE, w), lambda i: (i, 0))
    sds = lambda w, dt: jax.ShapeDtypeStruct((t, w), dt)
    tiles = pl.BlockSpec((nt, d_att, ATT_QB), lambda i: (i, 0, 0))
    tiles_sds = jax.ShapeDtypeStruct((t // ATT_QB, d_att, ATT_QB), BF16)
    return pl.pallas_call(
        functools.partial(_proj_kernel, d_rnn, d_att, d),
        out_shape=(sds(d_rnn, F32), sds(d_rnn, F32), sds(d_att, BF16), sds(d, F32), sds(d, F32),
                   tiles_sds, tiles_sds),
        grid=(t // TOKEN_TILE,),
        in_specs=[row(d), _const_spec((1, d)), _const_spec(w_main.shape),
                  _const_spec(w_qt.shape), _const_spec(w_vt.shape), _const_spec((1, 2 * d)),
                  _const_spec((HEAD_DIM, TOKEN_TILE)), _const_spec((1, LANES))],
        out_specs=(row(d_rnn), row(d_rnn), row(d_att), row(d), row(d), tiles, tiles),
        compiler_params=pltpu.CompilerParams(
            dimension_semantics=("parallel",), vmem_limit_bytes=56 * MIB),
        name="proj",
    )(x2d, g, w_main, w_qt, w_vt, gate_bias, q_gain_t, k_gain2)


def _rnn_kernel(xr_ref, yr_ref, cw_ref, cb_ref, wax_ref, ba_ref, bx_ref, lam_ref,
                za_ref, xbuf_ref, h_ref, a_ref, b_ref):
    rows, c = xr_ref.shape[1], xr_ref.shape[2]
    groups = rows // SUBLANES

    @pl.when(pl.program_id(1) == 0)
    def _():
        xbuf_ref[0:SUBLANES, :] = jnp.zeros((SUBLANES, c), F32)
        h_ref[...] = jnp.zeros((1, c), F32)

    xbuf_ref[SUBLANES:SUBLANES + rows, :] = xr_ref[0]
    u = cb_ref[...]
    for j in range(CONV_WIDTH):
        off = SUBLANES - (CONV_WIDTH - 1) + j
        u = u + xbuf_ref[off:off + rows, :] * cw_ref[j:j + 1, :]
    xbuf_ref[0:SUBLANES, :] = xbuf_ref[rows:rows + SUBLANES, :]

    gates = _dot(u.astype(BF16), wax_ref[...])
    r = jax.nn.sigmoid(gates[:, :c] + ba_ref[...])
    i = jax.nn.sigmoid(gates[:, c:] + bx_ref[...])
    log_a = (-LRU_C) * r * jax.nn.softplus(-lam_ref[...])
    a = jnp.exp(log_a)
    b = jnp.sqrt(-jnp.tanh(log_a) * (a * a + 1.0)) * (i * u)

    a = a.reshape(groups, SUBLANES, c)
    b = b.reshape(groups, SUBLANES, c)
    t_idx = lax.broadcasted_iota(jnp.int32, (groups, SUBLANES, c), 1)
    d = 1
    while d < SUBLANES:
        keep = t_idx >= d
        a_prev = jnp.where(keep, pltpu.roll(a, d, axis=1), 1.0)
        b_prev = jnp.where(keep, pltpu.roll(b, d, axis=1), 0.0)
        b = a * b_prev + b
        a = a * a_prev
        d *= 2
    a_ref[...] = a.reshape(rows, c)
    b_ref[...] = b.reshape(rows, c)

    h_prev = h_ref[...]
    for g in range(groups):
        sl = slice(g * SUBLANES, (g + 1) * SUBLANES)
        h = a_ref[sl, :] * h_prev + b_ref[sl, :]
        za_ref[0, sl, :] = (jax.nn.gelu(yr_ref[0, sl, :]) * h).astype(BF16)
        h_prev = h[SUBLANES - 1:SUBLANES, :]
    h_ref[...] = h_prev


def _rnn_call(xr, yr, conv_w, conv_b, w_ax, b_a, b_x, lam):
    bsz, s, c = xr.shape
    blk = pl.BlockSpec((1, TIME_TILE, c), lambda b, t: (b, t, 0))
    return pl.pallas_call(
        _rnn_kernel,
        out_shape=jax.ShapeDtypeStruct((bsz, s, c), BF16),
        grid=(bsz, s // TIME_TILE),
        in_specs=[blk, blk, _const_spec((CONV_WIDTH, c)), _const_spec((1, c)),
                  _const_spec((c, 2 * c)), _const_spec((1, c)), _const_spec((1, c)),
                  _const_spec((1, c))],
        out_specs=blk,
        scratch_shapes=[pltpu.VMEM((TIME_TILE + SUBLANES, c), F32),
                        pltpu.VMEM((1, c), F32),
                        pltpu.VMEM((TIME_TILE, c), F32),
                        pltpu.VMEM((TIME_TILE, c), F32)],
        compiler_params=pltpu.CompilerParams(
            dimension_semantics=("arbitrary", "arbitrary"), vmem_limit_bytes=48 * MIB),
        name="rnn",
    )(xr, yr, conv_w, conv_b, w_ax, b_a, b_x, lam)


def _bias_kernel(base_ref, o_ref):
    row = lax.broadcasted_iota(jnp.int32, (ATT_QB, 2 * ATT_QB), 0)
    for n in range(ATT_NBLK):
        x = jnp.broadcast_to(base_ref[0, n], (ATT_QB, 2 * ATT_QB))
        bit = 1
        while bit < ATT_QB:
            x = jnp.where((row & bit) != 0, pltpu.roll(x, bit, axis=1), x)
            bit *= 2
        o_ref[0, n] = x[:, :ATT_QB]


def _bias_call(rel_table):
    nh = rel_table.shape[0]
    j = np.arange(2 * ATT_QB)
    rel = np.where(j < ATT_QB, j, j - 2 * ATT_QB)[None, :] + ATT_QB * np.arange(ATT_NBLK)[:, None]
    idx = np.clip(rel, -REL_CLIP, REL_CLIP) + REL_CLIP
    base = (rel_table.astype(F32) * LOG2E)[:, idx].reshape(nh, ATT_NBLK, 1, 2 * ATT_QB)
    return pl.pallas_call(
        _bias_kernel,
        out_shape=jax.ShapeDtypeStruct((nh, ATT_NBLK, ATT_QB, ATT_QB), F32),
        grid=(nh,),
        in_specs=[pl.BlockSpec((1, ATT_NBLK, 1, 2 * ATT_QB), lambda h: (h, 0, 0, 0))],
        out_specs=pl.BlockSpec((1, ATT_NBLK, ATT_QB, ATT_QB), lambda h: (h, 0, 0, 0)),
        compiler_params=pltpu.CompilerParams(dimension_semantics=("parallel",)),
        name="bias",
    )(base)


def _slab_state(n, kc, half):
    per = ATT_QB // CHUNK
    ok = [0 <= per * n + qc - kc <= LEFT_CHUNKS
          for qc in range(half * (LANES // CHUNK), (half + 1) * (LANES // CHUNK))]
    return {(True, True): "all", (False, True): "hi", (True, False): "lo",
            (False, False): "none"}[tuple(ok)]


def _head_query(qt, h):
    lanes = ATT_HG * HEAD_DIM
    r0, r1 = h * HEAD_DIM, (h + 1) * HEAD_DIM
    pieces = []
    if r0:
        pieces.append(jnp.zeros((r0, ATT_QB), BF16))
    pieces.append(qt[r0:r1, :])
    if r1 < lanes:
        pieces.append(jnp.zeros((lanes - r1, ATT_QB), BF16))
    return jnp.concatenate(pieces, axis=0)


def _value_rows(vt_ref, tile, h):
    vt = vt_ref[0, tile, h * HEAD_DIM:(h + 1) * HEAD_DIM, :]
    return jnp.concatenate([vt, jnp.ones((2 * SUBLANES, ATT_QB), BF16)], axis=0)


def _attn_kernel(qt_ref, k_ref, vt_ref, bias_ref, mask_ref, o_ref):
    step = pl.program_id(2)
    halves = ATT_QB // LANES
    kchunks = ATT_QB // CHUNK

    def load_keys(tiles):
        starts = [t * ATT_QB if isinstance(t, int) else pl.multiple_of(t * ATT_QB, ATT_QB)
                  for t in tiles]
        return [k_ref[0, pl.ds(start, ATT_QB), :] for start in starts]

    def all_head_scores(j, tiles):
        qt = qt_ref[0, j]
        qblk = jnp.concatenate([_head_query(qt, h) for h in range(ATT_HG)], axis=1)
        return [_dot(kb, qblk) for kb in load_keys(tiles)]

    def column_max(x):
        m = []
        for a in range(halves):
            top = functools.reduce(jnp.maximum, [v for (_, _, aa), v in x.items() if aa == a])
            m.append(jnp.max(top, axis=0, keepdims=True))
        return m

    def interior_scores(j, tiles):
        s = all_head_scores(j, tiles)
        hi = lax.broadcasted_iota(jnp.int32, (CHUNK, LANES), 1) >= CHUNK
        heads = []
        for h in range(ATT_HG):
            x = {}
            for n in range(ATT_NBLK):
                for kc in range(kchunks):
                    for a in range(halves):
                        state = _slab_state(n, kc, a)
                        if state == "none":
                            continue
                        rs = slice(kc * CHUNK, (kc + 1) * CHUNK)
                        cs = slice(a * LANES, (a + 1) * LANES)
                        hs = slice(h * ATT_QB + a * LANES, h * ATT_QB + (a + 1) * LANES)
                        v = s[n][rs, hs] + bias_ref[h, n, rs, cs]
                        if state == "hi":
                            v = jnp.where(hi, v, -1e30)
                        elif state == "lo":
                            v = jnp.where(hi, -1e30, v)
                        x[n, kc, a] = v
            heads.append((x, column_max(x)))
        return heads

    def edge_scores(j, gi, tiles):
        s = all_head_scores(j, tiles)
        valid = [mask_ref[n] > (0.5 if gi >= n else 2.0) for n in range(ATT_NBLK)]
        heads = []
        for h in range(ATT_HG):
            x = {}
            for n in range(ATT_NBLK):
                xn = jnp.where(valid[n], s[n][:, h * ATT_QB:(h + 1) * ATT_QB] + bias_ref[h, n], -1e30)
                for kc in range(kchunks):
                    for a in range(halves):
                        x[n, kc, a] = xn[kc * CHUNK:(kc + 1) * CHUNK, a * LANES:(a + 1) * LANES]
            heads.append((x, column_max(x)))
        return heads

    def softmax_pv(j, tiles, heads):
        outs = []
        for h, (x, m) in enumerate(heads):
            o_ext = None
            for n in range(ATT_NBLK):
                rows = []
                for kc in range(kchunks):
                    rows.append(jnp.concatenate(
                        [jnp.exp2(x[n, kc, a] - m[a]).astype(BF16) if (n, kc, a) in x
                         else jnp.zeros((CHUNK, LANES), BF16) for a in range(halves)], axis=1))
                part = _dot(_value_rows(vt_ref, tiles[n], h), jnp.concatenate(rows, axis=0))
                o_ext = part if o_ext is None else o_ext + part
            outs.append(o_ext[:HEAD_DIM] * (1.0 / o_ext[HEAD_DIM:HEAD_DIM + 1]))
        ot = jnp.concatenate(outs, axis=0)
        o_ref[0, j * ATT_QB:(j + 1) * ATT_QB, :] = ot.T.astype(BF16)

    def run_units(first):
        base = 0 if first else step * ATT_SUB
        tiles = [[(max(j - n, 0) if first else base + j - n) for n in range(ATT_NBLK)]
                 for j in range(ATT_SUB)]

        def scores(j):
            if first and j < ATT_NBLK - 1:
                return edge_scores(j, j, tiles[j])
            return interior_scores(j, tiles[j])

        nxt = scores(0)
        for j in range(ATT_SUB):
            cur = nxt
            if j + 1 < ATT_SUB:
                nxt = scores(j + 1)
            softmax_pv(j, tiles[j], cur)

    @pl.when(step == 0)
    def _():
        run_units(True)

    @pl.when(step > 0)
    def _():
        run_units(False)


def _attn_call(qt, k, vt, bias, mask):
    bsz, s, d_att = k.shape
    lanes = ATT_HG * HEAD_DIM
    rows = ATT_SUB * ATT_QB
    nt = s // ATT_QB
    return pl.pallas_call(
        _attn_kernel,
        out_shape=jax.ShapeDtypeStruct((bsz, s, d_att), BF16),
        grid=(bsz, d_att // lanes, s // rows),
        in_specs=[pl.BlockSpec((1, ATT_SUB, lanes, ATT_QB), lambda b, g, i: (b, i, g, 0)),
                  pl.BlockSpec((1, s, lanes), lambda b, g, i: (b, 0, g)),
                  pl.BlockSpec((1, nt, lanes, ATT_QB), lambda b, g, i: (b, 0, g, 0)),
                  pl.BlockSpec((ATT_HG, ATT_NBLK, ATT_QB, ATT_QB), lambda b, g, i: (g, 0, 0, 0)),
                  _const_spec((ATT_NBLK, ATT_QB, ATT_QB))],
        out_specs=pl.BlockSpec((1, rows, lanes), lambda b, g, i: (b, i, g)),
        compiler_params=pltpu.CompilerParams(
            dimension_semantics=("parallel", "parallel", "arbitrary"),
            vmem_limit_bytes=48 * MIB),
        name="attn",
    )(qt, k, vt, bias, mask)


def _attn_valid_mask():
    per = ATT_QB // CHUNK
    n = np.arange(ATT_NBLK)[:, None, None]
    kc = (np.arange(ATT_QB) // CHUNK)[None, :, None]
    qc = (np.arange(ATT_QB) // CHUNK)[None, None, :]
    dist = per * n + qc - kc
    return ((dist >= 0) & (dist <= LEFT_CHUNKS)).astype(np.float32)


def _merge_kernel(x_ref, za_ref, at_ref, ga_ref, gb_ref, wa_ref, wb_ref, wo_ref,
                  g_ref, wg_ref, wu_ref, wd_ref, o_ref):
    y_a = _dot(za_ref[...], wa_ref[...])
    y_b = _dot(at_ref[...], wb_ref[...])
    mix = (ga_ref[...] * y_a + gb_ref[...] * y_b).astype(BF16)
    x = x_ref[...] + _dot(mix, wo_ref[...])
    o_ref[...] = _ffn_residual(x, g_ref, wg_ref, wu_ref, wd_ref)


def _merge_call(x2d, za, at, ga, gb, w_up_a, w_up_b, w_out, g, wg, wu, wd):
    t, d = x2d.shape
    d_rnn, d_att, d_ff = za.shape[1], at.shape[1], wg.shape[1]
    row = lambda w: pl.BlockSpec((TOKEN_TILE, w), lambda i: (i, 0))
    return pl.pallas_call(
        _merge_kernel,
        out_shape=jax.ShapeDtypeStruct((t, d), F32),
        grid=(t // TOKEN_TILE,),
        in_specs=[row(d), row(d_rnn), row(d_att), row(d), row(d),
                  _const_spec((d_rnn, d)), _const_spec((d_att, d)), _const_spec((d, d)),
                  _const_spec((1, d)), _const_spec((d, d_ff)), _const_spec((d, d_ff)),
                  _const_spec((d_ff, d))],
        out_specs=row(d),
        compiler_params=pltpu.CompilerParams(
            dimension_semantics=("parallel",), vmem_limit_bytes=56 * MIB),
        name="merge",
    )(x2d, za, at, ga, gb, w_up_a, w_up_b, w_out, g, wg, wu, wd)


def _block_diag(w):
    nb, n, _ = w.shape
    eye = jnp.eye(nb, dtype=w.dtype)
    return (eye[:, None, :, None] * w[:, :, None, :]).reshape(nb * n, nb * n)


def kernel(x, norm_ffn1, ffn1_w_gate, ffn1_w_up, ffn1_w_down, norm_mix, w_in, gate_bias,
           conv_w, conv_b, rg_w_a, rg_b_a, rg_w_x, rg_b_x, rg_lambda, w_up_a,
           q_gain, k_gain, rel_table, w_up_b, w_out,
           norm_ffn2, ffn2_w_gate, ffn2_w_up, ffn2_w_down):
    bsz, s, d = x.shape
    depth = w_in.shape[0]
    d_rnn = conv_w.shape[-1]
    d_att = w_up_b.shape[1]
    assert q_gain.shape[-1] == HEAD_DIM and rel_table.shape[-1] == 2 * REL_CLIP + 1
    assert (bsz * s) % TOKEN_TILE == 0 and s % TIME_TILE == 0 and s % (ATT_SUB * ATT_QB) == 0
    assert s % TOKEN_TILE == 0 and d_att == d

    mask = jnp.asarray(_attn_valid_mask())
    row = lambda a: a.reshape(1, -1).astype(F32)
    c_q, c_k, c_v, c_g = 2 * d_rnn, 2 * d_rnn + d_att, 2 * d_rnn + 2 * d_att, 2 * d_rnn + 3 * d_att
    q_scale = HEAD_DIM ** -0.5 * LOG2E

    x2d = x.reshape(bsz * s, d)
    for l in range(depth):
        x2d = _ffn_call(x2d, row(norm_ffn1[l]), ffn1_w_gate[l].astype(BF16),
                        ffn1_w_up[l].astype(BF16), ffn1_w_down[l].astype(BF16))
        w = w_in[l]
        w_main = jnp.concatenate([w[:, :c_q], w[:, c_k:c_v], w[:, c_g:]], axis=1).astype(BF16)
        w_qt = w[:, c_q:c_k].T.astype(BF16)
        w_vt = w[:, c_v:c_g].T.astype(BF16)
        q_gain_t = jnp.broadcast_to((q_gain[l].astype(F32) * q_scale)[:, None],
                                    (HEAD_DIM, TOKEN_TILE))
        k_gain2 = jnp.tile(row(k_gain[l]), (1, LANES // HEAD_DIM))
        xr, yr, k, ga, gb, qt, vt = _proj_call(
            x2d, row(norm_mix[l]), w_main, w_qt, w_vt, row(gate_bias[l]), q_gain_t, k_gain2,
            d_rnn, d_att)
        w_ax = jnp.concatenate([_block_diag(rg_w_a[l]), _block_diag(rg_w_x[l])],
                               axis=1).astype(BF16)
        za = _rnn_call(xr.reshape(bsz, s, d_rnn), yr.reshape(bsz, s, d_rnn),
                       conv_w[l].astype(F32), row(conv_b[l]), w_ax, row(rg_b_a[l]),
                       row(rg_b_x[l]), row(rg_lambda[l]))
        at = _attn_call(qt.reshape(bsz, s // ATT_QB, d_att, ATT_QB), k.reshape(bsz, s, d_att),
                        vt.reshape(bsz, s // ATT_QB, d_att, ATT_QB),
                        _bias_call(rel_table[l]), mask)
        x2d = _merge_call(x2d, za.reshape(bsz * s, d_rnn), at.reshape(bsz * s, d_att), ga, gb,
                          w_up_a[l].astype(BF16), w_up_b[l].astype(BF16), w_out[l].astype(BF16),
                          row(norm_ffn2[l]), ffn2_w_gate[l].astype(BF16),
                          ffn2_w_up[l].astype(BF16), ffn2_w_down[l].astype(BF16))
    return x2d.reshape(bsz, s, d)
```

```python
import functools

import numpy as np
import jax
import jax.numpy as jnp
from jax import lax
from jax.experimental import pallas as pl
from jax.experimental.pallas import tpu as pltpu

F32 = jnp.float32
BF16 = jnp.bfloat16

LANES = 128
SUBLANES = 8
MXU_DIM = 256
MIB = 1024 * 1024

CHUNK = 64
LEFT_CHUNKS = 8
HEAD_DIM = 64
REL_CLIP = 128
CONV_WIDTH = 4
LRU_C = 8.0
EPS = 1e-6
LOG2E = 1.4426950408889634

TOKEN_TILE = 512
FF_TILE = 3 * MXU_DIM
TIME_TILE = 256
ATT_QB = MXU_DIM
ATT_NBLK = LEFT_CHUNKS * CHUNK // ATT_QB + 1
ATT_SUB = 4
ATT_HG = MXU_DIM // HEAD_DIM
assert (LEFT_CHUNKS * CHUNK) % ATT_QB == 0 and ATT_QB % CHUNK == 0 and TOKEN_TILE % ATT_QB == 0


def _const_spec(shape):
    nd = len(shape)
    return pl.BlockSpec(shape, lambda *_: (0,) * nd, pipeline_mode=pl.Buffered(1))


def _layer_spec(layer, shape):
    nd = len(shape)
    return pl.BlockSpec((None,) + tuple(shape), lambda *_: (layer,) + (0,) * nd,
                        pipeline_mode=pl.Buffered(1))


def _rms_norm(x, g):
    ms = jnp.mean(x * x, axis=-1, keepdims=True)
    return x * lax.rsqrt(ms + EPS) * g


def _dot(a, b):
    return jnp.dot(a, b, preferred_element_type=F32)


def _dot_nt(a, b):
    return lax.dot_general(a, b, (((1,), (1,)), ((), ())), preferred_element_type=F32)


def _ffn_residual(x, g_ref, wg_ref, wu_ref, wd_ref):
    d_ff = wg_ref.shape[1]
    h = _rms_norm(x, g_ref[...]).astype(BF16)
    acc = None
    for c0 in range(0, d_ff, FF_TILE):
        c1 = min(c0 + FF_TILE, d_ff)
        gate = _dot(h, wg_ref[:, c0:c1])
        up = _dot(h, wu_ref[:, c0:c1])
        act = (gate * jax.nn.sigmoid(gate) * up).astype(BF16)
        part = _dot(act, wd_ref[c0:c1, :])
        acc = part if acc is None else acc + part
    return x + 0.5 * acc


def _ffn_kernel(x_ref, g_ref, wg_ref, wu_ref, wd_ref, o_ref):
    o_ref[...] = _ffn_residual(x_ref[...], g_ref, wg_ref, wu_ref, wd_ref)


def _ffn_call(x2d, g, wg, wu, wd, layer):
    t, d = x2d.shape
    d_ff = wg.shape[-1]
    row = pl.BlockSpec((TOKEN_TILE, d), lambda i: (i, 0))
    return pl.pallas_call(
        _ffn_kernel,
        out_shape=jax.ShapeDtypeStruct((t, d), F32),
        grid=(t // TOKEN_TILE,),
        in_specs=[row, _const_spec((1, d)), _layer_spec(layer, (d, d_ff)),
                  _layer_spec(layer, (d, d_ff)), _layer_spec(layer, (d_ff, d))],
        out_specs=row,
        compiler_params=pltpu.CompilerParams(
            dimension_semantics=("parallel",), vmem_limit_bytes=48 * MIB),
        name="ffn",
    )(x2d, g, wg, wu, wd)


def _head_rms_norm(x, gain2):
    rows, width = x.shape
    low = lax.broadcasted_iota(jnp.int32, (rows, LANES), 1) < HEAD_DIM
    out = []
    for c in range(0, width, LANES):
        xs = x[:, c:c + LANES]
        sq = xs * xs
        ms_lo = jnp.sum(jnp.where(low, sq, 0.0), axis=-1, keepdims=True) * (1.0 / HEAD_DIM)
        ms_hi = jnp.sum(jnp.where(low, 0.0, sq), axis=-1, keepdims=True) * (1.0 / HEAD_DIM)
        r = jnp.where(low, lax.rsqrt(ms_lo + EPS), lax.rsqrt(ms_hi + EPS))
        out.append(xs * r * gain2)
    return jnp.concatenate(out, axis=-1)


def _head_rms_norm_t(xt, gain_t):
    out = []
    for r0 in range(0, xt.shape[0], HEAD_DIM):
        xs = xt[r0:r0 + HEAD_DIM, :]
        ms = jnp.mean(xs * xs, axis=0, keepdims=True)
        out.append(xs * lax.rsqrt(ms + EPS) * gain_t)
    return jnp.concatenate(out, axis=0)


def _proj_kernel(d_rnn, d_att, d_model,
                 x_ref, g_ref, w_ref, wq_ref, wv_ref, gbias_ref, qg_ref, kg_ref, perm_ref,
                 xr_ref, yr_ref, k_ref, ga_ref, gb_ref, qt_ref, vt_ref):
    h = _rms_norm(x_ref[...], g_ref[...]).astype(BF16)
    c = 2 * d_rnn
    hp = jnp.concatenate([_dot(perm_ref[...], h[r0:r0 + TIME_TILE, :])
                          for r0 in range(0, h.shape[0], TIME_TILE)], axis=0).astype(BF16)
    xy = _dot(hp, w_ref[:, 0:c])
    xr_ref[...] = xy[:, :d_rnn]
    yr_ref[...] = xy[:, d_rnn:]
    c = c + d_att
    k = _dot(h, w_ref[:, c:c + d_att])
    k_ref[...] = _head_rms_norm(k, kg_ref[...]).astype(BF16)
    c = c + 2 * d_att
    ga = _dot(h, w_ref[:, c:c + d_model]) + gbias_ref[:, :d_model]
    ga_ref[...] = jax.nn.sigmoid(ga)
    gb = _dot(h, w_ref[:, c + d_model:c + 2 * d_model]) + gbias_ref[:, d_model:]
    gb_ref[...] = jax.nn.sigmoid(gb)
    qt = _head_rms_norm_t(_dot_nt(wq_ref[...], h), qg_ref[...]).astype(BF16)
    vt = _dot_nt(wv_ref[...], h).astype(BF16)
    for n in range(qt_ref.shape[0]):
        qt_ref[n] = qt[:, n * ATT_QB:(n + 1) * ATT_QB]
        vt_ref[n] = vt[:, n * ATT_QB:(n + 1) * ATT_QB]


def _proj_call(x2d, g, w_in, w_qt, w_vt, gate_bias, q_gain_t, k_gain2, perm, d_rnn, d_att,
               layer):
    t, d = x2d.shape
    nt = TOKEN_TILE // ATT_QB
    row = lambda w: pl.BlockSpec((TOKEN_TILE, w), lambda i: (i, 0))
    sds = lambda w, dt: jax.ShapeDtypeStruct((t, w), dt)
    tiles = pl.BlockSpec((nt, d_att, ATT_QB), lambda i: (i, 0, 0))
    tiles_sds = jax.ShapeDtypeStruct((t // ATT_QB, d_att, ATT_QB), BF16)
    return pl.pallas_call(
        functools.partial(_proj_kernel, d_rnn, d_att, d),
        out_shape=(sds(d_rnn, F32), sds(d_rnn, F32), sds(d_att, BF16), sds(d, F32), sds(d, F32),
                   tiles_sds, tiles_sds),
        grid=(t // TOKEN_TILE,),
        in_specs=[row(d), _const_spec((1, d)), _layer_spec(layer, w_in.shape[1:]),
                  _layer_spec(layer, w_qt.shape[1:]), _layer_spec(layer, w_vt.shape[1:]),
                  _const_spec((1, 2 * d)),
                  _const_spec((HEAD_DIM, TOKEN_TILE)), _const_spec((1, LANES)),
                  _const_spec(perm.shape)],
        out_specs=(row(d_rnn), row(d_rnn), row(d_att), row(d), row(d), tiles, tiles),
        compiler_params=pltpu.CompilerParams(
            dimension_semantics=("parallel",), vmem_limit_bytes=56 * MIB),
        name="proj",
    )(x2d, g, w_in, w_qt, w_vt, gate_bias, q_gain_t, k_gain2, perm)


def _rnn_kernel(n_blocks, xr_ref, yr_ref, cw_ref, cb_ref, wax_ref, ba_ref, bx_ref, lam_ref,
                unperm_ref, za_ref, tail_ref, h_ref, u_ref, a_ref, b_ref, z_ref):
    rows, c = xr_ref.shape[1], xr_ref.shape[2]
    seg = rows // SUBLANES
    taps = CONV_WIDTH - 1
    sub = lax.broadcasted_iota(jnp.int32, (SUBLANES, c), 0)

    @pl.when(pl.program_id(1) == 0)
    def _():
        tail_ref[...] = jnp.zeros((taps, SUBLANES, c), F32)
        h_ref[...] = jnp.zeros((1, c), F32)

    def group(ref, j):
        return ref[0, j * SUBLANES:(j + 1) * SUBLANES, :]

    last = [group(xr_ref, seg - k) for k in range(1, taps + 1)]
    hist = [jnp.where(sub == 0, pltpu.roll(tail_ref[k], 1, axis=0), pltpu.roll(last[k], 1, axis=0))
            for k in range(taps - 1, -1, -1)]
    for k in range(taps):
        tail_ref[k] = last[k]
    for j in range(seg):
        hist.append(last[seg - 1 - j] if seg - 1 - j < taps else group(xr_ref, j))
        u = cb_ref[...]
        for t in range(CONV_WIDTH):
            u = u + hist[t] * cw_ref[t]
        u_ref[j * SUBLANES:(j + 1) * SUBLANES, :] = u
        hist.pop(0)

    u_all = u_ref[...]
    u_bf = u_all.astype(BF16)
    n = c // n_blocks
    gates = []
    for base in (0, c):
        pieces = []
        for c0 in range(0, c, MXU_DIM):
            c1 = min(c0 + MXU_DIM, c)
            r0 = (c0 // n) * n // LANES * LANES
            r1 = min(-(-(-(-c1 // n) * n) // LANES) * LANES, c)
            pieces.append(_dot(u_bf[:, r0:r1], wax_ref[r0:r1, base + c0:base + c1]))
        gates.append(jnp.concatenate(pieces, axis=1))
    r = jax.nn.sigmoid(gates[0] + ba_ref[...])
    i = jax.nn.sigmoid(gates[1] + bx_ref[...])
    log_a = r * ((-LRU_C) * jax.nn.softplus(-lam_ref[...]))
    a = jnp.exp(log_a)
    a_ref[...] = a
    b_ref[...] = jnp.sqrt(-jnp.tanh(log_a) * (a * a + 1.0)) * (i * u_all)

    state = jnp.zeros((SUBLANES, c), F32)
    decay = jnp.ones((SUBLANES, c), F32)
    for j in range(seg):
        sl = slice(j * SUBLANES, (j + 1) * SUBLANES)
        a_j = a_ref[sl, :]
        state = a_j * state + b_ref[sl, :]
        decay = a_j * decay
        b_ref[sl, :] = state
        a_ref[sl, :] = decay

    d = 1
    while d < SUBLANES:
        keep = sub >= d
        decay_prev = jnp.where(keep, pltpu.roll(decay, d, axis=0), 1.0)
        state_prev = jnp.where(keep, pltpu.roll(state, d, axis=0), 0.0)
        state = decay * state_prev + state
        decay = decay * decay_prev
        d *= 2
    ends = decay * h_ref[...] + state
    carry = jnp.where(sub == 0, h_ref[...], pltpu.roll(ends, 1, axis=0))
    h_ref[...] = ends[SUBLANES - 1:SUBLANES, :]

    for j in range(seg):
        sl = slice(j * SUBLANES, (j + 1) * SUBLANES)
        z_ref[sl, :] = jax.nn.gelu(group(yr_ref, j)) * (b_ref[sl, :] + a_ref[sl, :] * carry)

    za_ref[0] = _dot(unperm_ref[...], z_ref[...].astype(BF16)).astype(BF16)


def _rnn_call(xr, yr, conv_w, conv_b, w_ax, b_a, b_x, lam, unperm, n_blocks, layer):
    bsz, s, c = xr.shape
    blk = pl.BlockSpec((1, TIME_TILE, c), lambda b, t: (b, t, 0))
    tile = pltpu.VMEM((TIME_TILE, c), F32)
    return pl.pallas_call(
        functools.partial(_rnn_kernel, n_blocks),
        out_shape=jax.ShapeDtypeStruct((bsz, s, c), BF16),
        grid=(bsz, s // TIME_TILE),
        in_specs=[blk, blk, _const_spec((CONV_WIDTH, SUBLANES, c)), _const_spec((SUBLANES, c)),
                  _layer_spec(layer, (c, 2 * c)), _const_spec((1, c)), _const_spec((1, c)),
                  _const_spec((1, c)), _const_spec(unperm.shape)],
        out_specs=blk,
        scratch_shapes=[pltpu.VMEM((CONV_WIDTH - 1, SUBLANES, c), F32),
                        pltpu.VMEM((1, c), F32), tile, tile, tile, tile],
        compiler_params=pltpu.CompilerParams(
            dimension_semantics=("arbitrary", "arbitrary"), vmem_limit_bytes=48 * MIB),
        name="rnn",
    )(xr, yr, conv_w, conv_b, w_ax, b_a, b_x, lam, unperm)


def _bias_kernel(base_ref, o_ref):
    row = lax.broadcasted_iota(jnp.int32, (ATT_QB, 2 * ATT_QB), 0)
    for n in range(ATT_NBLK):
        x = jnp.broadcast_to(base_ref[0, n], (ATT_QB, 2 * ATT_QB))
        bit = 1
        while bit < ATT_QB:
            x = jnp.where((row & bit) != 0, pltpu.roll(x, bit, axis=1), x)
            bit *= 2
        o_ref[0, n] = x[:, :ATT_QB]


def _bias_call(rel_table):
    nh = rel_table.shape[0]
    j = np.arange(2 * ATT_QB)
    rel = np.where(j < ATT_QB, j, j - 2 * ATT_QB)[None, :] + ATT_QB * np.arange(ATT_NBLK)[:, None]
    idx = np.clip(rel, -REL_CLIP, REL_CLIP) + REL_CLIP
    base = (rel_table.astype(F32) * LOG2E)[:, idx].reshape(nh, ATT_NBLK, 1, 2 * ATT_QB)
    return pl.pallas_call(
        _bias_kernel,
        out_shape=jax.ShapeDtypeStruct((nh, ATT_NBLK, ATT_QB, ATT_QB), F32),
        grid=(nh,),
        in_specs=[pl.BlockSpec((1, ATT_NBLK, 1, 2 * ATT_QB), lambda h: (h, 0, 0, 0))],
        out_specs=pl.BlockSpec((1, ATT_NBLK, ATT_QB, ATT_QB), lambda h: (h, 0, 0, 0)),
        compiler_params=pltpu.CompilerParams(dimension_semantics=("parallel",)),
        name="bias",
    )(base)


def _slab_state(n, kc, half):
    per = ATT_QB // CHUNK
    ok = [0 <= per * n + qc - kc <= LEFT_CHUNKS
          for qc in range(half * (LANES // CHUNK), (half + 1) * (LANES // CHUNK))]
    return {(True, True): "all", (False, True): "hi", (True, False): "lo",
            (False, False): "none"}[tuple(ok)]


def _head_query(qt, h):
    lanes = ATT_HG * HEAD_DIM
    r0, r1 = h * HEAD_DIM, (h + 1) * HEAD_DIM
    pieces = []
    if r0:
        pieces.append(jnp.zeros((r0, ATT_QB), BF16))
    pieces.append(qt[r0:r1, :])
    if r1 < lanes:
        pieces.append(jnp.zeros((lanes - r1, ATT_QB), BF16))
    return jnp.concatenate(pieces, axis=0)


def _value_rows(vt_ref, tile, h):
    vt = vt_ref[0, tile, h * HEAD_DIM:(h + 1) * HEAD_DIM, :]
    return jnp.concatenate([vt, jnp.ones((2 * SUBLANES, ATT_QB), BF16)], axis=0)


def _attn_kernel(qt_ref, k_ref, vt_ref, bias_ref, mask_ref, o_ref):
    step = pl.program_id(2)
    halves = ATT_QB // LANES
    kchunks = ATT_QB // CHUNK

    def load_keys(tiles):
        starts = [t * ATT_QB if isinstance(t, int) else pl.multiple_of(t * ATT_QB, ATT_QB)
                  for t in tiles]
        return [k_ref[0, pl.ds(start, ATT_QB), :] for start in starts]

    def all_head_scores(j, tiles):
        qt = qt_ref[0, j]
        qblk = jnp.concatenate([_head_query(qt, h) for h in range(ATT_HG)], axis=1)
        return [_dot(kb, qblk) for kb in load_keys(tiles)]

    def column_max(x):
        m = []
        for a in range(halves):
            top = functools.reduce(jnp.maximum, [v for (_, _, aa), v in x.items() if aa == a])
            m.append(jnp.max(top, axis=0, keepdims=True))
        return m

    def interior_scores(j, tiles):
        s = all_head_scores(j, tiles)
        hi = lax.broadcasted_iota(jnp.int32, (CHUNK, LANES), 1) >= CHUNK
        heads = []
        for h in range(ATT_HG):
            x = {}
            for n in range(ATT_NBLK):
                for kc in range(kchunks):
                    for a in range(halves):
                        state = _slab_state(n, kc, a)
                        if state == "none":
                            continue
                        rs = slice(kc * CHUNK, (kc + 1) * CHUNK)
                        cs = slice(a * LANES, (a + 1) * LANES)
                        hs = slice(h * ATT_QB + a * LANES, h * ATT_QB + (a + 1) * LANES)
                        v = s[n][rs, hs] + bias_ref[h, n, rs, cs]
                        if state == "hi":
                            v = jnp.where(hi, v, -1e30)
                        elif state == "lo":
                            v = jnp.where(hi, -1e30, v)
                        x[n, kc, a] = v
            heads.append((x, column_max(x)))
        return heads

    def edge_scores(j, gi, tiles):
        s = all_head_scores(j, tiles)
        valid = [mask_ref[n] > (0.5 if gi >= n else 2.0) for n in range(ATT_NBLK)]
        heads = []
        for h in range(ATT_HG):
            x = {}
            for n in range(ATT_NBLK):
                xn = jnp.where(valid[n], s[n][:, h * ATT_QB:(h + 1) * ATT_QB] + bias_ref[h, n], -1e30)
                for kc in range(kchunks):
                    for a in range(halves):
                        x[n, kc, a] = xn[kc * CHUNK:(kc + 1) * CHUNK, a * LANES:(a + 1) * LANES]
            heads.append((x, column_max(x)))
        return heads

    def softmax_pv(j, tiles, heads):
        outs = []
        for h, (x, m) in enumerate(heads):
            o_ext = None
            for n in range(ATT_NBLK):
                rows = []
                for kc in range(kchunks):
                    rows.append(jnp.concatenate(
                        [jnp.exp2(x[n, kc, a] - m[a]).astype(BF16) if (n, kc, a) in x
                         else jnp.zeros((CHUNK, LANES), BF16) for a in range(halves)], axis=1))
                part = _dot(_value_rows(vt_ref, tiles[n], h), jnp.concatenate(rows, axis=0))
                o_ext = part if o_ext is None else o_ext + part
            outs.append(o_ext[:HEAD_DIM] * (1.0 / o_ext[HEAD_DIM:HEAD_DIM + 1]))
        ot = jnp.concatenate(outs, axis=0)
        o_ref[0, j * ATT_QB:(j + 1) * ATT_QB, :] = ot.T.astype(BF16)

    def run_units(first):
        base = 0 if first else step * ATT_SUB
        tiles = [[(max(j - n, 0) if first else base + j - n) for n in range(ATT_NBLK)]
                 for j in range(ATT_SUB)]

        def scores(j):
            if first and j < ATT_NBLK - 1:
                return edge_scores(j, j, tiles[j])
            return interior_scores(j, tiles[j])

        nxt = scores(0)
        for j in range(ATT_SUB):
            cur = nxt
            if j + 1 < ATT_SUB:
                nxt = scores(j + 1)
            softmax_pv(j, tiles[j], cur)

    @pl.when(step == 0)
    def _():
        run_units(True)

    @pl.when(step > 0)
    def _():
        run_units(False)


def _attn_call(qt, k, vt, bias, mask):
    bsz, s, d_att = k.shape
    lanes = ATT_HG * HEAD_DIM
    rows = ATT_SUB * ATT_QB
    nt = s // ATT_QB
    return pl.pallas_call(
        _attn_kernel,
        out_shape=jax.ShapeDtypeStruct((bsz, s, d_att), BF16),
        grid=(bsz, d_att // lanes, s // rows),
        in_specs=[pl.BlockSpec((1, ATT_SUB, lanes, ATT_QB), lambda b, g, i: (b, i, g, 0)),
                  pl.BlockSpec((1, s, lanes), lambda b, g, i: (b, 0, g)),
                  pl.BlockSpec((1, nt, lanes, ATT_QB), lambda b, g, i: (b, 0, g, 0)),
                  pl.BlockSpec((ATT_HG, ATT_NBLK, ATT_QB, ATT_QB), lambda b, g, i: (g, 0, 0, 0)),
                  _const_spec((ATT_NBLK, ATT_QB, ATT_QB))],
        out_specs=pl.BlockSpec((1, rows, lanes), lambda b, g, i: (b, i, g)),
        compiler_params=pltpu.CompilerParams(
            dimension_semantics=("parallel", "parallel", "arbitrary"),
            vmem_limit_bytes=48 * MIB),
        name="attn",
    )(qt, k, vt, bias, mask)


def _attn_valid_mask():
    per = ATT_QB // CHUNK
    n = np.arange(ATT_NBLK)[:, None, None]
    kc = (np.arange(ATT_QB) // CHUNK)[None, :, None]
    qc = (np.arange(ATT_QB) // CHUNK)[None, None, :]
    dist = per * n + qc - kc
    return ((dist >= 0) & (dist <= LEFT_CHUNKS)).astype(np.float32)


def _merge_kernel(x_ref, za_ref, at_ref, ga_ref, gb_ref, wa_ref, wb_ref, wo_ref,
                  g_ref, wg_ref, wu_ref, wd_ref, o_ref):
    y_a = _dot(za_ref[...], wa_ref[...])
    y_b = _dot(at_ref[...], wb_ref[...])
    mix = (ga_ref[...] * y_a + gb_ref[...] * y_b).astype(BF16)
    x = x_ref[...] + _dot(mix, wo_ref[...])
    o_ref[...] = _ffn_residual(x, g_ref, wg_ref, wu_ref, wd_ref)


def _merge_call(x2d, za, at, ga, gb, w_up_a, w_up_b, w_out, g, wg, wu, wd, layer):
    t, d = x2d.shape
    d_rnn, d_att, d_ff = za.shape[1], at.shape[1], wg.shape[-1]
    row = lambda w: pl.BlockSpec((TOKEN_TILE, w), lambda i: (i, 0))
    return pl.pallas_call(
        _merge_kernel,
        out_shape=jax.ShapeDtypeStruct((t, d), F32),
        grid=(t // TOKEN_TILE,),
        in_specs=[row(d), row(d_rnn), row(d_att), row(d), row(d),
                  _layer_spec(layer, (d_rnn, d)), _layer_spec(layer, (d_att, d)),
                  _layer_spec(layer, (d, d)), _const_spec((1, d)), _layer_spec(layer, (d, d_ff)),
                  _layer_spec(layer, (d, d_ff)), _layer_spec(layer, (d_ff, d))],
        out_specs=row(d),
        compiler_params=pltpu.CompilerParams(
            dimension_semantics=("parallel",), vmem_limit_bytes=56 * MIB),
        name="merge",
    )(x2d, za, at, ga, gb, w_up_a, w_up_b, w_out, g, wg, wu, wd)


def _segment_permutation():
    seg = TIME_TILE // SUBLANES
    p = np.arange(TIME_TILE)
    perm = np.zeros((TIME_TILE, TIME_TILE), np.float32)
    perm[p, (p % SUBLANES) * seg + p // SUBLANES] = 1.0
    return perm


def _block_diag(w):
    depth, nb, n, _ = w.shape
    tiled = jnp.tile(w.reshape(depth, nb * n, n), (1, 1, nb))
    blk = np.arange(nb * n) // n
    return jnp.where(jnp.asarray(blk[:, None] == blk[None, :]), tiled, 0)


def kernel(x, norm_ffn1, ffn1_w_gate, ffn1_w_up, ffn1_w_down, norm_mix, w_in, gate_bias,
           conv_w, conv_b, rg_w_a, rg_b_a, rg_w_x, rg_b_x, rg_lambda, w_up_a,
           q_gain, k_gain, rel_table, w_up_b, w_out,
           norm_ffn2, ffn2_w_gate, ffn2_w_up, ffn2_w_down):
    bsz, s, d = x.shape
    depth = w_in.shape[0]
    d_rnn = conv_w.shape[-1]
    d_att = w_up_b.shape[1]
    assert q_gain.shape[-1] == HEAD_DIM and rel_table.shape[-1] == 2 * REL_CLIP + 1
    assert (bsz * s) % TOKEN_TILE == 0 and s % TIME_TILE == 0 and s % (ATT_SUB * ATT_QB) == 0
    assert s % TOKEN_TILE == 0 and d_att == d and TOKEN_TILE % TIME_TILE == 0

    mask = jnp.asarray(_attn_valid_mask())
    perm = _segment_permutation()
    perm, unperm = jnp.asarray(perm, BF16), jnp.asarray(perm.T, BF16)
    row = lambda a: a.reshape(1, -1).astype(F32)
    bf = lambda a: a.astype(BF16)
    rows8 = lambda a: jnp.broadcast_to(a.astype(F32), a.shape[:-2] + (SUBLANES, a.shape[-1]))
    q_scale = HEAD_DIM ** -0.5 * LOG2E

    c_q, c_v = 2 * d_rnn, 2 * d_rnn + 2 * d_att
    w_in_b = bf(w_in)
    w_qt = bf(jnp.swapaxes(w_in[:, :, c_q:c_q + d_att], 1, 2))
    w_vt = bf(jnp.swapaxes(w_in[:, :, c_v:c_v + d_att], 1, 2))
    w_ax = bf(jnp.concatenate([_block_diag(rg_w_a), _block_diag(rg_w_x)], axis=2))
    ffn1 = (bf(ffn1_w_gate), bf(ffn1_w_up), bf(ffn1_w_down))
    ffn2 = (bf(ffn2_w_gate), bf(ffn2_w_up), bf(ffn2_w_down))
    w_up_a_b, w_up_b_b, w_out_b = bf(w_up_a), bf(w_up_b), bf(w_out)

    x2d = x.reshape(bsz * s, d)
    for l in range(depth):
        x2d = _ffn_call(x2d, row(norm_ffn1[l]), *ffn1, l)
        q_gain_t = jnp.broadcast_to((q_gain[l].astype(F32) * q_scale)[:, None],
                                    (HEAD_DIM, TOKEN_TILE))
        k_gain2 = jnp.tile(row(k_gain[l]), (1, LANES // HEAD_DIM))
        xr, yr, k, ga, gb, qt, vt = _proj_call(
            x2d, row(norm_mix[l]), w_in_b, w_qt, w_vt, row(gate_bias[l]), q_gain_t, k_gain2,
            perm, d_rnn, d_att, l)
        za = _rnn_call(xr.reshape(bsz, s, d_rnn), yr.reshape(bsz, s, d_rnn),
                       rows8(conv_w[l][:, None, :]), rows8(conv_b[l][None, :]), w_ax, row(rg_b_a[l]),
                       row(rg_b_x[l]), row(rg_lambda[l]), unperm, rg_w_a.shape[1], l)
        at = _attn_call(qt.reshape(bsz, s // ATT_QB, d_att, ATT_QB), k.reshape(bsz, s, d_att),
                        vt.reshape(bsz, s // ATT_QB, d_att, ATT_QB),
                        _bias_call(rel_table[l]), mask)
        x2d = _merge_call(x2d, za.reshape(bsz * s, d_rnn), at.reshape(bsz * s, d_att), ga, gb,
                          w_up_a_b, w_up_b_b, w_out_b, row(norm_ffn2[l]), *ffn2, l)
    return x2d.reshape(bsz, s, d)
```

```python
import functools

import numpy as np
import jax
import jax.numpy as jnp
from jax import lax
from jax.experimental import pallas as pl
from jax.experimental.pallas import tpu as pltpu

F32 = jnp.float32
BF16 = jnp.bfloat16

LANES = 128
SUBLANES = 8
MXU_DIM = 256
MIB = 1024 * 1024

CHUNK = 64
LEFT_CHUNKS = 8
HEAD_DIM = 64
REL_CLIP = 128
CONV_WIDTH = 4
LRU_C = 8.0
EPS = 1e-6
LOG2E = 1.4426950408889634

TOKEN_TILE = 512
FF_TILE = 3 * MXU_DIM
TIME_TILE = 256
ATT_QB = MXU_DIM
ATT_NBLK = LEFT_CHUNKS * CHUNK // ATT_QB + 1
ATT_SUB = 4
ATT_HG = MXU_DIM // HEAD_DIM
assert (LEFT_CHUNKS * CHUNK) % ATT_QB == 0 and ATT_QB % CHUNK == 0 and TOKEN_TILE % ATT_QB == 0


def _const_spec(shape):
    nd = len(shape)
    return pl.BlockSpec(shape, lambda *_: (0,) * nd, pipeline_mode=pl.Buffered(1))


def _layer_spec(layer, shape):
    nd = len(shape)
    return pl.BlockSpec((None,) + tuple(shape), lambda *_: (layer,) + (0,) * nd,
                        pipeline_mode=pl.Buffered(1))


def _rms_norm(x, g):
    ms = jnp.mean(x * x, axis=-1, keepdims=True)
    return x * lax.rsqrt(ms + EPS) * g


def _dot(a, b):
    return jnp.dot(a, b, preferred_element_type=F32)


def _dot_tt(a, b):
    return lax.dot_general(a, b, (((0,), (1,)), ((), ())), preferred_element_type=F32)


def _ffn_residual(x, g_ref, wg_ref, wu_ref, wd_ref):
    d_ff = wg_ref.shape[1]
    h = _rms_norm(x, g_ref[...]).astype(BF16)
    acc = None
    for c0 in range(0, d_ff, FF_TILE):
        c1 = min(c0 + FF_TILE, d_ff)
        gate = _dot(h, wg_ref[:, c0:c1])
        up = _dot(h, wu_ref[:, c0:c1])
        act = (gate * jax.nn.sigmoid(gate) * up).astype(BF16)
        part = _dot(act, wd_ref[c0:c1, :])
        acc = part if acc is None else acc + part
    return x + 0.5 * acc


def _ffn_kernel(x_ref, g_ref, wg_ref, wu_ref, wd_ref, o_ref):
    o_ref[...] = _ffn_residual(x_ref[...], g_ref, wg_ref, wu_ref, wd_ref)


def _ffn_call(x2d, g, wg, wu, wd, layer):
    t, d = x2d.shape
    d_ff = wg.shape[-1]
    row = pl.BlockSpec((TOKEN_TILE, d), lambda i: (i, 0))
    return pl.pallas_call(
        _ffn_kernel,
        out_shape=jax.ShapeDtypeStruct((t, d), F32),
        grid=(t // TOKEN_TILE,),
        in_specs=[row, _const_spec((1, d)), _layer_spec(layer, (d, d_ff)),
                  _layer_spec(layer, (d, d_ff)), _layer_spec(layer, (d_ff, d))],
        out_specs=row,
        compiler_params=pltpu.CompilerParams(
            dimension_semantics=("parallel",), vmem_limit_bytes=48 * MIB),
        name="ffn",
    )(x2d, g, wg, wu, wd)


def _head_rms_norm(x, gain2):
    rows, width = x.shape
    low = lax.broadcasted_iota(jnp.int32, (rows, LANES), 1) < HEAD_DIM
    out = []
    for c in range(0, width, LANES):
        xs = x[:, c:c + LANES]
        sq = xs * xs
        ms_lo = jnp.sum(jnp.where(low, sq, 0.0), axis=-1, keepdims=True) * (1.0 / HEAD_DIM)
        ms_hi = jnp.sum(jnp.where(low, 0.0, sq), axis=-1, keepdims=True) * (1.0 / HEAD_DIM)
        r = jnp.where(low, lax.rsqrt(ms_lo + EPS), lax.rsqrt(ms_hi + EPS))
        out.append(xs * r * gain2)
    return jnp.concatenate(out, axis=-1)


def _head_rms_norm_t(xt, gain_t):
    out = []
    for r0 in range(0, xt.shape[0], HEAD_DIM):
        xs = xt[r0:r0 + HEAD_DIM, :]
        ms = jnp.mean(xs * xs, axis=0, keepdims=True)
        out.append(xs * lax.rsqrt(ms + EPS) * gain_t)
    return jnp.concatenate(out, axis=0)


def _proj_kernel(d_rnn, d_att, d_model,
                 x_ref, g_ref, w_ref, gbias_ref, qg_ref, kg_ref, perm_ref,
                 xr_ref, yr_ref, k_ref, ga_ref, gb_ref, qt_ref, vt_ref):
    h = _rms_norm(x_ref[...], g_ref[...]).astype(BF16)
    c = 2 * d_rnn
    hp = jnp.concatenate([_dot(perm_ref[...], h[r0:r0 + TIME_TILE, :])
                          for r0 in range(0, h.shape[0], TIME_TILE)], axis=0).astype(BF16)
    xy = _dot(hp, w_ref[:, 0:c])
    xr_ref[...] = xy[:, :d_rnn]
    yr_ref[...] = xy[:, d_rnn:]
    c = c + d_att
    k = _dot(h, w_ref[:, c:c + d_att])
    k_ref[...] = _head_rms_norm(k, kg_ref[...]).astype(BF16)
    c = c + 2 * d_att
    ga = _dot(h, w_ref[:, c:c + d_model]) + gbias_ref[:, :d_model]
    ga_ref[...] = jax.nn.sigmoid(ga)
    gb = _dot(h, w_ref[:, c + d_model:c + 2 * d_model]) + gbias_ref[:, d_model:]
    gb_ref[...] = jax.nn.sigmoid(gb)
    c_q, c_v = 2 * d_rnn, 2 * d_rnn + 2 * d_att
    qt = _head_rms_norm_t(_dot_tt(w_ref[:, c_q:c_q + d_att], h), qg_ref[...]).astype(BF16)
    vt = _dot_tt(w_ref[:, c_v:c_v + d_att], h).astype(BF16)
    for n in range(qt_ref.shape[0]):
        qt_ref[n] = qt[:, n * ATT_QB:(n + 1) * ATT_QB]
        vt_ref[n] = vt[:, n * ATT_QB:(n + 1) * ATT_QB]


def _proj_call(x2d, g, w_in, gate_bias, q_gain_t, k_gain2, perm, d_rnn, d_att, layer):
    t, d = x2d.shape
    nt = TOKEN_TILE // ATT_QB
    row = lambda w: pl.BlockSpec((TOKEN_TILE, w), lambda i: (i, 0))
    sds = lambda w, dt: jax.ShapeDtypeStruct((t, w), dt)
    tiles = pl.BlockSpec((nt, d_att, ATT_QB), lambda i: (i, 0, 0))
    tiles_sds = jax.ShapeDtypeStruct((t // ATT_QB, d_att, ATT_QB), BF16)
    return pl.pallas_call(
        functools.partial(_proj_kernel, d_rnn, d_att, d),
        out_shape=(sds(d_rnn, F32), sds(d_rnn, F32), sds(d_att, BF16), sds(d, F32), sds(d, F32),
                   tiles_sds, tiles_sds),
        grid=(t // TOKEN_TILE,),
        in_specs=[row(d), _const_spec((1, d)), _layer_spec(layer, w_in.shape[1:]),
                  _const_spec((1, 2 * d)),
                  _const_spec((HEAD_DIM, TOKEN_TILE)), _const_spec((1, LANES)),
                  _const_spec(perm.shape)],
        out_specs=(row(d_rnn), row(d_rnn), row(d_att), row(d), row(d), tiles, tiles),
        compiler_params=pltpu.CompilerParams(
            dimension_semantics=("parallel",), vmem_limit_bytes=56 * MIB),
        name="proj",
    )(x2d, g, w_in, gate_bias, q_gain_t, k_gain2, perm)


def _rnn_kernel(n_blocks, xr_ref, yr_ref, cw_ref, cb_ref, wax_ref, ba_ref, bx_ref, lam_ref,
                unperm_ref, za_ref, tail_ref, h_ref, u_ref, a_ref, b_ref, z_ref):
    rows, c = xr_ref.shape[1], xr_ref.shape[2]
    seg = rows // SUBLANES
    taps = CONV_WIDTH - 1
    sub = lax.broadcasted_iota(jnp.int32, (SUBLANES, c), 0)

    @pl.when(pl.program_id(1) == 0)
    def _():
        tail_ref[...] = jnp.zeros((taps, SUBLANES, c), F32)
        h_ref[...] = jnp.zeros((1, c), F32)

    def group(ref, j):
        return ref[0, j * SUBLANES:(j + 1) * SUBLANES, :]

    last = [group(xr_ref, seg - k) for k in range(1, taps + 1)]
    hist = [jnp.where(sub == 0, pltpu.roll(tail_ref[k], 1, axis=0), pltpu.roll(last[k], 1, axis=0))
            for k in range(taps - 1, -1, -1)]
    for k in range(taps):
        tail_ref[k] = last[k]
    for j in range(seg):
        hist.append(last[seg - 1 - j] if seg - 1 - j < taps else group(xr_ref, j))
        u = cb_ref[...]
        for t in range(CONV_WIDTH):
            u = u + hist[t] * cw_ref[t]
        u_ref[j * SUBLANES:(j + 1) * SUBLANES, :] = u
        hist.pop(0)

    u_all = u_ref[...]
    u_bf = u_all.astype(BF16)
    n = c // n_blocks
    gates = []
    for base in (0, c):
        pieces = []
        for c0 in range(0, c, MXU_DIM):
            c1 = min(c0 + MXU_DIM, c)
            r0 = (c0 // n) * n // LANES * LANES
            r1 = min(-(-(-(-c1 // n) * n) // LANES) * LANES, c)
            pieces.append(_dot(u_bf[:, r0:r1], wax_ref[r0:r1, base + c0:base + c1]))
        gates.append(jnp.concatenate(pieces, axis=1))
    r = jax.nn.sigmoid(gates[0] + ba_ref[...])
    i = jax.nn.sigmoid(gates[1] + bx_ref[...])
    log_a = r * ((-LRU_C) * jax.nn.softplus(-lam_ref[...]))
    a = jnp.exp(log_a)
    a_ref[...] = a
    b_ref[...] = jnp.sqrt(-jnp.tanh(log_a) * (a * a + 1.0)) * (i * u_all)

    state = jnp.zeros((SUBLANES, c), F32)
    decay = jnp.ones((SUBLANES, c), F32)
    for j in range(seg):
        sl = slice(j * SUBLANES, (j + 1) * SUBLANES)
        a_j = a_ref[sl, :]
        state = a_j * state + b_ref[sl, :]
        decay = a_j * decay
        b_ref[sl, :] = state
        a_ref[sl, :] = decay

    d = 1
    while d < SUBLANES:
        keep = sub >= d
        decay_prev = jnp.where(keep, pltpu.roll(decay, d, axis=0), 1.0)
        state_prev = jnp.where(keep, pltpu.roll(state, d, axis=0), 0.0)
        state = decay * state_prev + state
        decay = decay * decay_prev
        d *= 2
    ends = decay * h_ref[...] + state
    carry = jnp.where(sub == 0, h_ref[...], pltpu.roll(ends, 1, axis=0))
    h_ref[...] = ends[SUBLANES - 1:SUBLANES, :]

    for j in range(seg):
        sl = slice(j * SUBLANES, (j + 1) * SUBLANES)
        z_ref[sl, :] = jax.nn.gelu(group(yr_ref, j)) * (b_ref[sl, :] + a_ref[sl, :] * carry)

    za_ref[0] = _dot(unperm_ref[...], z_ref[...].astype(BF16)).astype(BF16)


def _rnn_call(xr, yr, conv_w, conv_b, w_ax, b_a, b_x, lam, unperm, n_blocks, layer):
    bsz, s, c = xr.shape
    blk = pl.BlockSpec((1, TIME_TILE, c), lambda b, t: (b, t, 0))
    tile = pltpu.VMEM((TIME_TILE, c), F32)
    return pl.pallas_call(
        functools.partial(_rnn_kernel, n_blocks),
        out_shape=jax.ShapeDtypeStruct((bsz, s, c), BF16),
        grid=(bsz, s // TIME_TILE),
        in_specs=[blk, blk, _const_spec((CONV_WIDTH, SUBLANES, c)), _const_spec((SUBLANES, c)),
                  _layer_spec(layer, (c, 2 * c)), _const_spec((1, c)), _const_spec((1, c)),
                  _const_spec((1, c)), _const_spec(unperm.shape)],
        out_specs=blk,
        scratch_shapes=[pltpu.VMEM((CONV_WIDTH - 1, SUBLANES, c), F32),
                        pltpu.VMEM((1, c), F32), tile, tile, tile, tile],
        compiler_params=pltpu.CompilerParams(
            dimension_semantics=("arbitrary", "arbitrary"), vmem_limit_bytes=48 * MIB),
        name="rnn",
    )(xr, yr, conv_w, conv_b, w_ax, b_a, b_x, lam, unperm)


def _bias_kernel(base_ref, o_ref):
    row = lax.broadcasted_iota(jnp.int32, (ATT_QB, 2 * ATT_QB), 0)
    for n in range(ATT_NBLK):
        if ATT_QB * n - (ATT_QB - 1) >= REL_CLIP:
            o_ref[0, n] = jnp.broadcast_to(base_ref[0, n][:, :ATT_QB], (ATT_QB, ATT_QB))
            continue
        x = jnp.broadcast_to(base_ref[0, n], (ATT_QB, 2 * ATT_QB))
        bit = 1
        while bit < ATT_QB:
            x = jnp.where((row & bit) != 0, pltpu.roll(x, bit, axis=1), x)
            bit *= 2
        o_ref[0, n] = x[:, :ATT_QB]


def _bias_call(rel_table):
    nh = rel_table.shape[0]
    j = np.arange(2 * ATT_QB)
    rel = np.where(j < ATT_QB, j, j - 2 * ATT_QB)[None, :] + ATT_QB * np.arange(ATT_NBLK)[:, None]
    idx = np.clip(rel, -REL_CLIP, REL_CLIP) + REL_CLIP
    base = (rel_table.astype(F32) * LOG2E)[:, idx].reshape(nh, ATT_NBLK, 1, 2 * ATT_QB)
    return pl.pallas_call(
        _bias_kernel,
        out_shape=jax.ShapeDtypeStruct((nh, ATT_NBLK, ATT_QB, ATT_QB), F32),
        grid=(nh,),
        in_specs=[pl.BlockSpec((1, ATT_NBLK, 1, 2 * ATT_QB), lambda h: (h, 0, 0, 0))],
        out_specs=pl.BlockSpec((1, ATT_NBLK, ATT_QB, ATT_QB), lambda h: (h, 0, 0, 0)),
        compiler_params=pltpu.CompilerParams(dimension_semantics=("parallel",)),
        name="bias",
    )(base)


def _slab_state(n, kc, half):
    per = ATT_QB // CHUNK
    ok = [0 <= per * n + qc - kc <= LEFT_CHUNKS
          for qc in range(half * (LANES // CHUNK), (half + 1) * (LANES // CHUNK))]
    return {(True, True): "all", (False, True): "hi", (True, False): "lo",
            (False, False): "none"}[tuple(ok)]


def _head_query(qt, h):
    lanes = ATT_HG * HEAD_DIM
    r0, r1 = h * HEAD_DIM, (h + 1) * HEAD_DIM
    pieces = []
    if r0:
        pieces.append(jnp.zeros((r0, ATT_QB), BF16))
    pieces.append(qt[r0:r1, :])
    if r1 < lanes:
        pieces.append(jnp.zeros((lanes - r1, ATT_QB), BF16))
    return jnp.concatenate(pieces, axis=0)


def _value_rows(vt_ref, tile, h):
    vt = vt_ref[0, tile, h * HEAD_DIM:(h + 1) * HEAD_DIM, :]
    return jnp.concatenate([vt, jnp.ones((2 * SUBLANES, ATT_QB), BF16)], axis=0)


def _attn_kernel(qt_ref, k_ref, vt_ref, bias_ref, mask_ref, o_ref):
    step = pl.program_id(2)
    halves = ATT_QB // LANES
    kchunks = ATT_QB // CHUNK

    def load_keys(tiles):
        starts = [t * ATT_QB if isinstance(t, int) else pl.multiple_of(t * ATT_QB, ATT_QB)
                  for t in tiles]
        return [k_ref[0, pl.ds(start, ATT_QB), :] for start in starts]

    def all_head_scores(j, tiles):
        qt = qt_ref[0, j]
        qblk = jnp.concatenate([_head_query(qt, h) for h in range(ATT_HG)], axis=1)
        return [_dot(kb, qblk) for kb in load_keys(tiles)]

    def column_max(x):
        m = []
        for a in range(halves):
            top = functools.reduce(jnp.maximum, [v for (_, _, aa), v in x.items() if aa == a])
            m.append(jnp.max(top, axis=0, keepdims=True))
        return m

    def interior_scores(j, tiles):
        s = all_head_scores(j, tiles)
        hi = lax.broadcasted_iota(jnp.int32, (CHUNK, LANES), 1) >= CHUNK
        heads = []
        for h in range(ATT_HG):
            x = {}
            for n in range(ATT_NBLK):
                for kc in range(kchunks):
                    for a in range(halves):
                        state = _slab_state(n, kc, a)
                        if state == "none":
                            continue
                        rs = slice(kc * CHUNK, (kc + 1) * CHUNK)
                        cs = slice(a * LANES, (a + 1) * LANES)
                        hs = slice(h * ATT_QB + a * LANES, h * ATT_QB + (a + 1) * LANES)
                        v = s[n][rs, hs] + bias_ref[h, n, rs, cs]
                        if state == "hi":
                            v = jnp.where(hi, v, -1e30)
                        elif state == "lo":
                            v = jnp.where(hi, -1e30, v)
                        x[n, kc, a] = v
            heads.append((x, column_max(x)))
        return heads

    def edge_scores(j, gi, tiles):
        s = all_head_scores(j, tiles)
        valid = [mask_ref[n] > (0.5 if gi >= n else 2.0) for n in range(ATT_NBLK)]
        heads = []
        for h in range(ATT_HG):
            x = {}
            for n in range(ATT_NBLK):
                xn = jnp.where(valid[n], s[n][:, h * ATT_QB:(h + 1) * ATT_QB] + bias_ref[h, n], -1e30)
                for kc in range(kchunks):
                    for a in range(halves):
                        x[n, kc, a] = xn[kc * CHUNK:(kc + 1) * CHUNK, a * LANES:(a + 1) * LANES]
            heads.append((x, column_max(x)))
        return heads

    def softmax_pv(j, tiles, heads):
        outs = []
        for h, (x, m) in enumerate(heads):
            o_ext = None
            for n in range(ATT_NBLK):
                rows = []
                for kc in range(kchunks):
                    rows.append(jnp.concatenate(
                        [jnp.exp2((x[n, kc, a] - m[a]).astype(BF16)) if (n, kc, a) in x
                         else jnp.zeros((CHUNK, LANES), BF16) for a in range(halves)], axis=1))
                part = _dot(_value_rows(vt_ref, tiles[n], h), jnp.concatenate(rows, axis=0))
                o_ext = part if o_ext is None else o_ext + part
            outs.append(o_ext[:HEAD_DIM] * (1.0 / o_ext[HEAD_DIM:HEAD_DIM + 1]))
        ot = jnp.concatenate(outs, axis=0)
        o_ref[0, j * ATT_QB:(j + 1) * ATT_QB, :] = ot.T.astype(BF16)

    def run_units(first):
        base = 0 if first else step * ATT_SUB
        tiles = [[(max(j - n, 0) if first else base + j - n) for n in range(ATT_NBLK)]
                 for j in range(ATT_SUB)]

        def scores(j):
            if first and j < ATT_NBLK - 1:
                return edge_scores(j, j, tiles[j])
            return interior_scores(j, tiles[j])

        nxt = scores(0)
        for j in range(ATT_SUB):
            cur = nxt
            if j + 1 < ATT_SUB:
                nxt = scores(j + 1)
            softmax_pv(j, tiles[j], cur)

    @pl.when(step == 0)
    def _():
        run_units(True)

    @pl.when(step > 0)
    def _():
        run_units(False)


def _attn_call(qt, k, vt, bias, mask, layer):
    bsz, s, d_att = k.shape
    lanes = ATT_HG * HEAD_DIM
    rows = ATT_SUB * ATT_QB
    nt = s // ATT_QB
    return pl.pallas_call(
        _attn_kernel,
        out_shape=jax.ShapeDtypeStruct((bsz, s, d_att), BF16),
        grid=(bsz, d_att // lanes, s // rows),
        in_specs=[pl.BlockSpec((1, ATT_SUB, lanes, ATT_QB), lambda b, g, i: (b, i, g, 0)),
                  pl.BlockSpec((1, s, lanes), lambda b, g, i: (b, 0, g)),
                  pl.BlockSpec((1, nt, lanes, ATT_QB), lambda b, g, i: (b, 0, g, 0)),
                  pl.BlockSpec((ATT_HG, ATT_NBLK, ATT_QB, ATT_QB),
                               lambda b, g, i: (layer * (d_att // lanes) + g, 0, 0, 0)),
                  _const_spec((ATT_NBLK, ATT_QB, ATT_QB))],
        out_specs=pl.BlockSpec((1, rows, lanes), lambda b, g, i: (b, i, g)),
        compiler_params=pltpu.CompilerParams(
            dimension_semantics=("parallel", "parallel", "arbitrary"),
            vmem_limit_bytes=48 * MIB),
        name="attn",
    )(qt, k, vt, bias, mask)


def _attn_valid_mask():
    per = ATT_QB // CHUNK
    n = np.arange(ATT_NBLK)[:, None, None]
    kc = (np.arange(ATT_QB) // CHUNK)[None, :, None]
    qc = (np.arange(ATT_QB) // CHUNK)[None, None, :]
    dist = per * n + qc - kc
    return ((dist >= 0) & (dist <= LEFT_CHUNKS)).astype(np.float32)


def _merge_kernel(x_ref, za_ref, at_ref, ga_ref, gb_ref, wa_ref, wb_ref, wo_ref,
                  g_ref, wg_ref, wu_ref, wd_ref, o_ref):
    y_a = _dot(za_ref[...], wa_ref[...])
    y_b = _dot(at_ref[...], wb_ref[...])
    mix = (ga_ref[...] * y_a + gb_ref[...] * y_b).astype(BF16)
    x = x_ref[...] + _dot(mix, wo_ref[...])
    o_ref[...] = _ffn_residual(x, g_ref, wg_ref, wu_ref, wd_ref)


def _merge_call(x2d, za, at, ga, gb, w_up_a, w_up_b, w_out, g, wg, wu, wd, layer):
    t, d = x2d.shape
    d_rnn, d_att, d_ff = za.shape[1], at.shape[1], wg.shape[-1]
    row = lambda w: pl.BlockSpec((TOKEN_TILE, w), lambda i: (i, 0))
    return pl.pallas_call(
        _merge_kernel,
        out_shape=jax.ShapeDtypeStruct((t, d), F32),
        grid=(t // TOKEN_TILE,),
        in_specs=[row(d), row(d_rnn), row(d_att), row(d), row(d),
                  _layer_spec(layer, (d_rnn, d)), _layer_spec(layer, (d_att, d)),
                  _layer_spec(layer, (d, d)), _const_spec((1, d)), _layer_spec(layer, (d, d_ff)),
                  _layer_spec(layer, (d, d_ff)), _layer_spec(layer, (d_ff, d))],
        out_specs=row(d),
        compiler_params=pltpu.CompilerParams(
            dimension_semantics=("parallel",), vmem_limit_bytes=56 * MIB),
        name="merge",
    )(x2d, za, at, ga, gb, w_up_a, w_up_b, w_out, g, wg, wu, wd)


def _segment_permutation():
    seg = TIME_TILE // SUBLANES
    p = np.arange(TIME_TILE)
    perm = np.zeros((TIME_TILE, TIME_TILE), np.float32)
    perm[p, (p % SUBLANES) * seg + p // SUBLANES] = 1.0
    return perm


def _block_diag(w):
    depth, nb, n, _ = w.shape
    tiled = jnp.tile(w.reshape(depth, nb * n, n), (1, 1, nb))
    blk = np.arange(nb * n) // n
    return jnp.where(jnp.asarray(blk[:, None] == blk[None, :]), tiled, 0)


def kernel(x, norm_ffn1, ffn1_w_gate, ffn1_w_up, ffn1_w_down, norm_mix, w_in, gate_bias,
           conv_w, conv_b, rg_w_a, rg_b_a, rg_w_x, rg_b_x, rg_lambda, w_up_a,
           q_gain, k_gain, rel_table, w_up_b, w_out,
           norm_ffn2, ffn2_w_gate, ffn2_w_up, ffn2_w_down):
    bsz, s, d = x.shape
    depth = w_in.shape[0]
    d_rnn = conv_w.shape[-1]
    d_att = w_up_b.shape[1]
    assert q_gain.shape[-1] == HEAD_DIM and rel_table.shape[-1] == 2 * REL_CLIP + 1
    assert (bsz * s) % TOKEN_TILE == 0 and s % TIME_TILE == 0 and s % (ATT_SUB * ATT_QB) == 0
    assert s % TOKEN_TILE == 0 and d_att == d and TOKEN_TILE % TIME_TILE == 0

    mask = jnp.asarray(_attn_valid_mask())
    perm = _segment_permutation()
    perm, unperm = jnp.asarray(perm, BF16), jnp.asarray(perm.T, BF16)
    row = lambda a: a.reshape(1, -1).astype(F32)
    bf = lambda a: a.astype(BF16)
    rows8 = lambda a: jnp.broadcast_to(a.astype(F32), a.shape[:-2] + (SUBLANES, a.shape[-1]))
    q_scale = HEAD_DIM ** -0.5 * LOG2E

    w_in_b = bf(w_in)
    bias = _bias_call(rel_table.reshape(-1, rel_table.shape[-1]))
    w_ax = bf(jnp.concatenate([_block_diag(rg_w_a), _block_diag(rg_w_x)], axis=2))
    ffn1 = (bf(ffn1_w_gate), bf(ffn1_w_up), bf(ffn1_w_down))
    ffn2 = (bf(ffn2_w_gate), bf(ffn2_w_up), bf(ffn2_w_down))
    w_up_a_b, w_up_b_b, w_out_b = bf(w_up_a), bf(w_up_b), bf(w_out)

    x2d = x.reshape(bsz * s, d)
    for l in range(depth):
        x2d = _ffn_call(x2d, row(norm_ffn1[l]), *ffn1, l)
        q_gain_t = jnp.broadcast_to((q_gain[l].astype(F32) * q_scale)[:, None],
                                    (HEAD_DIM, TOKEN_TILE))
        k_gain2 = jnp.tile(row(k_gain[l]), (1, LANES // HEAD_DIM))
        xr, yr, k, ga, gb, qt, vt = _proj_call(
            x2d, row(norm_mix[l]), w_in_b, row(gate_bias[l]), q_gain_t, k_gain2,
            perm, d_rnn, d_att, l)
        za = _rnn_call(xr.reshape(bsz, s, d_rnn), yr.reshape(bsz, s, d_rnn),
                       rows8(conv_w[l][:, None, :]), rows8(conv_b[l][None, :]), w_ax, row(rg_b_a[l]),
                       row(rg_b_x[l]), row(rg_lambda[l]), unperm, rg_w_a.shape[1], l)
        at = _attn_call(qt.reshape(bsz, s // ATT_QB, d_att, ATT_QB), k.reshape(bsz, s, d_att),
                        vt.reshape(bsz, s // ATT_QB, d_att, ATT_QB),
                        bias, mask, l)
        x2d = _merge_call(x2d, za.reshape(bsz * s, d_rnn), at.reshape(bsz * s, d_att), ga, gb,
                          w_up_a_b, w_up_b_b, w_out_b, row(norm_ffn2[l]), *ffn2, l)
    return x2d.reshape(bsz, s, d)
```

```python
import functools

import numpy as np
import jax
import jax.numpy as jnp
from jax import lax
from jax.experimental import pallas as pl
from jax.experimental.pallas import tpu as pltpu

F32 = jnp.float32
BF16 = jnp.bfloat16

LANES = 128
SUBLANES = 8
MXU_DIM = 256
MIB = 1024 * 1024

CHUNK = 64
LEFT_CHUNKS = 8
HEAD_DIM = 64
REL_CLIP = 128
CONV_WIDTH = 4
LRU_C = 8.0
EPS = 1e-6
LOG2E = 1.4426950408889634

TOKEN_TILE = 512
FF_TILE = 3 * MXU_DIM
TIME_TILE = 256
ATT_QB = MXU_DIM
ATT_NBLK = LEFT_CHUNKS * CHUNK // ATT_QB + 1
ATT_SUB = 8
ATT_HG = MXU_DIM // HEAD_DIM
assert (LEFT_CHUNKS * CHUNK) % ATT_QB == 0 and ATT_QB % CHUNK == 0 and TOKEN_TILE % ATT_QB == 0

def _const_spec(shape):
    nd = len(shape)
    return pl.BlockSpec(shape, lambda *_: (0,) * nd, pipeline_mode=pl.Buffered(1))


def _layer_spec(layer, shape):
    nd = len(shape)
    return pl.BlockSpec((None,) + tuple(shape), lambda *_: (layer,) + (0,) * nd,
                        pipeline_mode=pl.Buffered(1))


def _rms_norm(x, g):
    ms = jnp.mean(x * x, axis=-1, keepdims=True)
    return x * lax.rsqrt(ms + EPS) * g


def _dot(a, b):
    return jnp.dot(a, b, preferred_element_type=F32)


def _dot_tt(a, b):
    return lax.dot_general(a, b, (((0,), (1,)), ((), ())), preferred_element_type=F32)


def _ffn_residual(x, g_ref, wg_ref, wu_ref, wd_ref):
    d_ff = wg_ref.shape[1]
    h = _rms_norm(x, g_ref[...]).astype(BF16)
    acc = None
    for c0 in range(0, d_ff, FF_TILE):
        c1 = min(c0 + FF_TILE, d_ff)
        gate = _dot(h, wg_ref[:, c0:c1])
        up = _dot(h, wu_ref[:, c0:c1])
        act = (gate * jax.nn.sigmoid(gate) * up).astype(BF16)
        part = _dot(act, wd_ref[c0:c1, :])
        acc = part if acc is None else acc + part
    return x + 0.5 * acc


def _ffn_kernel(x_ref, g_ref, wg_ref, wu_ref, wd_ref, o_ref):
    o_ref[...] = _ffn_residual(x_ref[...], g_ref, wg_ref, wu_ref, wd_ref)


def _ffn_call(x2d, g, wg, wu, wd, layer):
    t, d = x2d.shape
    d_ff = wg.shape[-1]
    row = pl.BlockSpec((TOKEN_TILE, d), lambda i: (i, 0))
    return pl.pallas_call(
        _ffn_kernel,
        out_shape=jax.ShapeDtypeStruct((t, d), F32),
        grid=(t // TOKEN_TILE,),
        in_specs=[row, _const_spec((1, d)), _layer_spec(layer, (d, d_ff)),
                  _layer_spec(layer, (d, d_ff)), _layer_spec(layer, (d_ff, d))],
        out_specs=row,
        compiler_params=pltpu.CompilerParams(
            dimension_semantics=("parallel",), vmem_limit_bytes=48 * MIB),
        name="ffn",
    )(x2d, g, wg, wu, wd)


def _head_rms_norm(x, gain2):
    rows, width = x.shape
    low = lax.broadcasted_iota(jnp.int32, (rows, LANES), 1) < HEAD_DIM
    out = []
    for c in range(0, width, LANES):
        xs = x[:, c:c + LANES]
        sq = xs * xs
        ms_lo = jnp.sum(jnp.where(low, sq, 0.0), axis=-1, keepdims=True) * (1.0 / HEAD_DIM)
        ms_hi = jnp.sum(jnp.where(low, 0.0, sq), axis=-1, keepdims=True) * (1.0 / HEAD_DIM)
        r = jnp.where(low, lax.rsqrt(ms_lo + EPS), lax.rsqrt(ms_hi + EPS))
        out.append(xs * r * gain2)
    return jnp.concatenate(out, axis=-1)


def _head_rms_norm_t(xt, gain_t):
    out = []
    for r0 in range(0, xt.shape[0], HEAD_DIM):
        xs = xt[r0:r0 + HEAD_DIM, :]
        ms = jnp.mean(xs * xs, axis=0, keepdims=True)
        out.append(xs * lax.rsqrt(ms + EPS) * gain_t)
    return jnp.concatenate(out, axis=0)


def _proj_kernel(d_rnn, d_att, d_model,
                 x_ref, g_ref, w_ref, gbias_ref, qg_ref, kg_ref, perm_ref,
                 xr_ref, yr_ref, k_ref, ga_ref, gb_ref, qt_ref, vt_ref):
    h = _rms_norm(x_ref[...], g_ref[...]).astype(BF16)
    c = 2 * d_rnn
    hp = jnp.concatenate([_dot(perm_ref[...], h[r0:r0 + TIME_TILE, :])
                          for r0 in range(0, h.shape[0], TIME_TILE)], axis=0).astype(BF16)
    xy = _dot(hp, w_ref[:, 0:c])
    xr_ref[...] = xy[:, :d_rnn]
    yr_ref[...] = xy[:, d_rnn:]
    c = c + d_att
    k = _dot(h, w_ref[:, c:c + d_att])
    k_ref[...] = _head_rms_norm(k, kg_ref[...]).astype(BF16)
    c = c + 2 * d_att
    ga = _dot(h, w_ref[:, c:c + d_model]) + gbias_ref[:, :d_model]
    ga_ref[...] = jax.nn.sigmoid(ga)
    gb = _dot(h, w_ref[:, c + d_model:c + 2 * d_model]) + gbias_ref[:, d_model:]
    gb_ref[...] = jax.nn.sigmoid(gb)
    c_q, c_v = 2 * d_rnn, 2 * d_rnn + 2 * d_att
    qt = _head_rms_norm_t(_dot_tt(w_ref[:, c_q:c_q + d_att], h), qg_ref[...]).astype(BF16)
    vt = _dot_tt(w_ref[:, c_v:c_v + d_att], h).astype(BF16)
    for n in range(qt_ref.shape[0]):
        qt_ref[n] = qt[:, n * ATT_QB:(n + 1) * ATT_QB]
        vt_ref[n] = vt[:, n * ATT_QB:(n + 1) * ATT_QB]


def _proj_call(x2d, g, w_in, gate_bias, q_gain_t, k_gain2, perm, d_rnn, d_att, layer):
    t, d = x2d.shape
    nt = TOKEN_TILE // ATT_QB
    row = lambda w: pl.BlockSpec((TOKEN_TILE, w), lambda i: (i, 0))
    sds = lambda w, dt: jax.ShapeDtypeStruct((t, w), dt)
    tiles = pl.BlockSpec((nt, d_att, ATT_QB), lambda i: (i, 0, 0))
    tiles_sds = jax.ShapeDtypeStruct((t // ATT_QB, d_att, ATT_QB), BF16)
    return pl.pallas_call(
        functools.partial(_proj_kernel, d_rnn, d_att, d),
        out_shape=(sds(d_rnn, F32), sds(d_rnn, F32), sds(d_att, BF16), sds(d, F32), sds(d, F32),
                   tiles_sds, tiles_sds),
        grid=(t // TOKEN_TILE,),
        in_specs=[row(d), _const_spec((1, d)), _layer_spec(layer, w_in.shape[1:]),
                  _const_spec((1, 2 * d)),
                  _const_spec((HEAD_DIM, TOKEN_TILE)), _const_spec((1, LANES)),
                  _const_spec(perm.shape)],
        out_specs=(row(d_rnn), row(d_rnn), row(d_att), row(d), row(d), tiles, tiles),
        compiler_params=pltpu.CompilerParams(
            dimension_semantics=("parallel",), vmem_limit_bytes=56 * MIB),
        name="proj",
    )(x2d, g, w_in, gate_bias, q_gain_t, k_gain2, perm)


def _rnn_kernel(n_blocks, xr_ref, yr_ref, cw_ref, cb_ref, wax_ref, ba_ref, bx_ref, lam_ref,
                unperm_ref, za_ref, tail_ref, h_ref, u_ref, a_ref, b_ref, z_ref):
    rows, c = xr_ref.shape[1], xr_ref.shape[2]
    seg = rows // SUBLANES
    taps = CONV_WIDTH - 1
    sub = lax.broadcasted_iota(jnp.int32, (SUBLANES, c), 0)

    @pl.when(pl.program_id(1) == 0)
    def _():
        tail_ref[...] = jnp.zeros((taps, SUBLANES, c), F32)
        h_ref[...] = jnp.zeros((1, c), F32)

    def group(ref, j):
        return ref[0, j * SUBLANES:(j + 1) * SUBLANES, :]

    last = [group(xr_ref, seg - k) for k in range(1, taps + 1)]
    hist = [jnp.where(sub == 0, pltpu.roll(tail_ref[k], 1, axis=0), pltpu.roll(last[k], 1, axis=0))
            for k in range(taps - 1, -1, -1)]
    for k in range(taps):
        tail_ref[k] = last[k]
    for j in range(seg):
        hist.append(last[seg - 1 - j] if seg - 1 - j < taps else group(xr_ref, j))
        u = cb_ref[...]
        for t in range(CONV_WIDTH):
            u = u + hist[t] * cw_ref[t]
        u_ref[j * SUBLANES:(j + 1) * SUBLANES, :] = u
        hist.pop(0)

    u_all = u_ref[...]
    u_bf = u_all.astype(BF16)
    n = c // n_blocks
    gates = []
    for base in (0, c):
        pieces = []
        for c0 in range(0, c, MXU_DIM):
            c1 = min(c0 + MXU_DIM, c)
            r0 = (c0 // n) * n // LANES * LANES
            r1 = min(-(-(-(-c1 // n) * n) // LANES) * LANES, c)
            pieces.append(_dot(u_bf[:, r0:r1], wax_ref[r0:r1, base + c0:base + c1]))
        gates.append(jnp.concatenate(pieces, axis=1))
    r = jax.nn.sigmoid(gates[0] + ba_ref[...])
    i = jax.nn.sigmoid(gates[1] + bx_ref[...])
    log_a = r * ((-LRU_C) * jax.nn.softplus(-lam_ref[...]))
    a = jnp.exp(log_a)
    a_ref[...] = a
    b_ref[...] = jnp.sqrt(-jnp.tanh(log_a) * (a * a + 1.0)) * (i * u_all)

    state = jnp.zeros((SUBLANES, c), F32)
    decay = jnp.ones((SUBLANES, c), F32)
    for j in range(seg):
        sl = slice(j * SUBLANES, (j + 1) * SUBLANES)
        a_j = a_ref[sl, :]
        state = a_j * state + b_ref[sl, :]
        decay = a_j * decay
        b_ref[sl, :] = state
        a_ref[sl, :] = decay

    d = 1
    while d < SUBLANES:
        keep = sub >= d
        decay_prev = jnp.where(keep, pltpu.roll(decay, d, axis=0), 1.0)
        state_prev = jnp.where(keep, pltpu.roll(state, d, axis=0), 0.0)
        state = decay * state_prev + state
        decay = decay * decay_prev
        d *= 2
    ends = decay * h_ref[...] + state
    carry = jnp.where(sub == 0, h_ref[...], pltpu.roll(ends, 1, axis=0))
    h_ref[...] = ends[SUBLANES - 1:SUBLANES, :]

    for j in range(seg):
        sl = slice(j * SUBLANES, (j + 1) * SUBLANES)
        z_ref[sl, :] = jax.nn.gelu(group(yr_ref, j)) * (b_ref[sl, :] + a_ref[sl, :] * carry)

    za_ref[0] = _dot(unperm_ref[...], z_ref[...].astype(BF16)).astype(BF16)


def _rnn_call(xr, yr, conv_w, conv_b, w_ax, b_a, b_x, lam, unperm, n_blocks, layer):
    bsz, s, c = xr.shape
    blk = pl.BlockSpec((1, TIME_TILE, c), lambda b, t: (b, t, 0))
    tile = pltpu.VMEM((TIME_TILE, c), F32)
    return pl.pallas_call(
        functools.partial(_rnn_kernel, n_blocks),
        out_shape=jax.ShapeDtypeStruct((bsz, s, c), BF16),
        grid=(bsz, s // TIME_TILE),
        in_specs=[blk, blk, _const_spec((CONV_WIDTH, SUBLANES, c)), _const_spec((SUBLANES, c)),
                  _layer_spec(layer, (c, 2 * c)), _const_spec((1, c)), _const_spec((1, c)),
                  _const_spec((1, c)), _const_spec(unperm.shape)],
        out_specs=blk,
        scratch_shapes=[pltpu.VMEM((CONV_WIDTH - 1, SUBLANES, c), F32),
                        pltpu.VMEM((1, c), F32), tile, tile, tile, tile],
        compiler_params=pltpu.CompilerParams(
            dimension_semantics=("arbitrary", "arbitrary"), vmem_limit_bytes=48 * MIB),
        name="rnn",
    )(xr, yr, conv_w, conv_b, w_ax, b_a, b_x, lam, unperm)


def _bias_kernel(base_ref, o_ref):
    for h in range(o_ref.shape[0]):
        for n in range(ATT_NBLK):
            if ATT_QB * n - (ATT_QB - 1) >= REL_CLIP:
                o_ref[h, n] = jnp.broadcast_to(base_ref[h, n][:, :ATT_QB], (ATT_QB, ATT_QB))
                continue
            x = jnp.broadcast_to(base_ref[h, n], (ATT_QB, 2 * ATT_QB))
            o_ref[h, n] = pltpu.roll(x, 0, 1, stride=1, stride_axis=0)[:, :ATT_QB]


def _bias_call(rel_table):
    nh = rel_table.shape[0]
    j = np.arange(2 * ATT_QB)
    rel = np.where(j < ATT_QB, j, j - 2 * ATT_QB)[None, :] + ATT_QB * np.arange(ATT_NBLK)[:, None]
    idx = np.clip(rel, -REL_CLIP, REL_CLIP) + REL_CLIP
    base = (rel_table.astype(F32) * LOG2E)[:, idx].reshape(nh, ATT_NBLK, 1, 2 * ATT_QB)
    return pl.pallas_call(
        _bias_kernel,
        out_shape=jax.ShapeDtypeStruct((nh, ATT_NBLK, ATT_QB, ATT_QB), F32),
        grid=(nh // ATT_HG,),
        in_specs=[pl.BlockSpec((ATT_HG, ATT_NBLK, 1, 2 * ATT_QB), lambda h: (h, 0, 0, 0))],
        out_specs=pl.BlockSpec((ATT_HG, ATT_NBLK, ATT_QB, ATT_QB), lambda h: (h, 0, 0, 0)),
        compiler_params=pltpu.CompilerParams(dimension_semantics=("parallel",)),
        name="bias",
    )(base)


def _slab_state(n, kc, half):
    per = ATT_QB // CHUNK
    ok = [0 <= per * n + qc - kc <= LEFT_CHUNKS
          for qc in range(half * (LANES // CHUNK), (half + 1) * (LANES // CHUNK))]
    return {(True, True): "all", (False, True): "hi", (True, False): "lo",
            (False, False): "none"}[tuple(ok)]


def _head_query(qt, h):
    lanes = ATT_HG * HEAD_DIM
    r0, r1 = h * HEAD_DIM, (h + 1) * HEAD_DIM
    pieces = []
    if r0:
        pieces.append(jnp.zeros((r0, ATT_QB), BF16))
    pieces.append(qt[r0:r1, :])
    if r1 < lanes:
        pieces.append(jnp.zeros((lanes - r1, ATT_QB), BF16))
    return jnp.concatenate(pieces, axis=0)


def _value_rows(vt_ref, tile, h):
    vt = vt_ref[0, tile, h * HEAD_DIM:(h + 1) * HEAD_DIM, :]
    return jnp.concatenate([vt, jnp.ones((2 * SUBLANES, ATT_QB), BF16)], axis=0)


def _attn_kernel(qt_ref, k_ref, vt_ref, bias_ref, mask_ref, o_ref):
    step = pl.program_id(2)
    halves = ATT_QB // LANES
    kchunks = ATT_QB // CHUNK

    def load_keys(tiles):
        starts = [t * ATT_QB if isinstance(t, int) else pl.multiple_of(t * ATT_QB, ATT_QB)
                  for t in tiles]
        return [k_ref[0, pl.ds(start, ATT_QB), :] for start in starts]

    def head_scores(j, tiles, h):
        qm = _head_query(qt_ref[0, j], h)
        return [_dot(kb, qm) for kb in load_keys(tiles)]

    def column_max(x):
        m = []
        for a in range(halves):
            top = functools.reduce(jnp.maximum, [v for (_, _, aa), v in x.items() if aa == a])
            m.append(jnp.max(top, axis=0, keepdims=True))
        return m

    def interior_scores(j, tiles, h):
        s = head_scores(j, tiles, h)
        hi = lax.broadcasted_iota(jnp.int32, (CHUNK, LANES), 1) >= CHUNK
        x = {}
        for n in range(ATT_NBLK):
            for kc in range(kchunks):
                for a in range(halves):
                    state = _slab_state(n, kc, a)
                    if state == "none":
                        continue
                    rs = slice(kc * CHUNK, (kc + 1) * CHUNK)
                    cs = slice(a * LANES, (a + 1) * LANES)
                    v = s[n][rs, cs] + bias_ref[h, n, rs, cs]
                    if state == "hi":
                        v = jnp.where(hi, v, -1e30)
                    elif state == "lo":
                        v = jnp.where(hi, -1e30, v)
                    x[n, kc, a] = v
        return x, column_max(x)

    def edge_scores(j, gi, tiles, h):
        s = head_scores(j, tiles, h)
        x = {}
        for n in range(ATT_NBLK):
            valid = mask_ref[n] > (0.5 if gi >= n else 2.0)
            xn = jnp.where(valid, s[n] + bias_ref[h, n], -1e30)
            for kc in range(kchunks):
                for a in range(halves):
                    x[n, kc, a] = xn[kc * CHUNK:(kc + 1) * CHUNK, a * LANES:(a + 1) * LANES]
        return x, column_max(x)

    def softmax_pv(tiles, h, x, m):
        o_ext = None
        for n in range(ATT_NBLK):
            rows = []
            for kc in range(kchunks):
                rows.append(jnp.concatenate(
                    [jnp.exp2((x[n, kc, a] - m[a]).astype(BF16)) if (n, kc, a) in x
                     else jnp.zeros((CHUNK, LANES), BF16) for a in range(halves)], axis=1))
            part = _dot(_value_rows(vt_ref, tiles[n], h), jnp.concatenate(rows, axis=0))
            o_ext = part if o_ext is None else o_ext + part
        return o_ext[:HEAD_DIM] * (1.0 / o_ext[HEAD_DIM:HEAD_DIM + 1])

    def run_units(first):
        base = 0 if first else step * ATT_SUB
        tiles = [[(max(j - n, 0) if first else base + j - n) for n in range(ATT_NBLK)]
                 for j in range(ATT_SUB)]

        def scores(j, h):
            if first and j < ATT_NBLK - 1:
                return edge_scores(j, j, tiles[j], h)
            return interior_scores(j, tiles[j], h)

        pending = [scores(0, h) for h in range(ATT_HG)]
        for j in range(ATT_SUB):
            outs = []
            for h in range(ATT_HG):
                x, m = pending[h]
                if j + 1 < ATT_SUB:
                    pending[h] = scores(j + 1, h)
                outs.append(softmax_pv(tiles[j], h, x, m))
            ot = jnp.concatenate(outs, axis=0)
            o_ref[0, j * ATT_QB:(j + 1) * ATT_QB, :] = ot.T.astype(BF16)

    @pl.when(step == 0)
    def _():
        run_units(True)

    @pl.when(step > 0)
    def _():
        run_units(False)


def _attn_call(qt, k, vt, bias, mask, layer):
    bsz, s, d_att = k.shape
    lanes = ATT_HG * HEAD_DIM
    rows = ATT_SUB * ATT_QB
    nt = s // ATT_QB
    return pl.pallas_call(
        _attn_kernel,
        out_shape=jax.ShapeDtypeStruct((bsz, s, d_att), BF16),
        grid=(bsz, d_att // lanes, s // rows),
        in_specs=[pl.BlockSpec((1, ATT_SUB, lanes, ATT_QB), lambda b, g, i: (b, i, g, 0)),
                  pl.BlockSpec((1, s, lanes), lambda b, g, i: (b, 0, g)),
                  pl.BlockSpec((1, nt, lanes, ATT_QB), lambda b, g, i: (b, 0, g, 0)),
                  pl.BlockSpec((ATT_HG, ATT_NBLK, ATT_QB, ATT_QB),
                               lambda b, g, i: (layer * (d_att // lanes) + g, 0, 0, 0)),
                  _const_spec((ATT_NBLK, ATT_QB, ATT_QB))],
        out_specs=pl.BlockSpec((1, rows, lanes), lambda b, g, i: (b, i, g)),
        compiler_params=pltpu.CompilerParams(
            dimension_semantics=("parallel", "parallel", "arbitrary"),
            vmem_limit_bytes=48 * MIB),
        name="attn",
    )(qt, k, vt, bias, mask)


def _attn_valid_mask():
    per = ATT_QB // CHUNK
    n = np.arange(ATT_NBLK)[:, None, None]
    kc = (np.arange(ATT_QB) // CHUNK)[None, :, None]
    qc = (np.arange(ATT_QB) // CHUNK)[None, None, :]
    dist = per * n + qc - kc
    return ((dist >= 0) & (dist <= LEFT_CHUNKS)).astype(np.float32)


def _merge_kernel(x_ref, za_ref, at_ref, ga_ref, gb_ref, wa_ref, wb_ref, wo_ref,
                  g_ref, wg_ref, wu_ref, wd_ref, o_ref):
    y_a = _dot(za_ref[...], wa_ref[...])
    y_b = _dot(at_ref[...], wb_ref[...])
    mix = (ga_ref[...] * y_a + gb_ref[...] * y_b).astype(BF16)
    x = x_ref[...] + _dot(mix, wo_ref[...])
    o_ref[...] = _ffn_residual(x, g_ref, wg_ref, wu_ref, wd_ref)


def _merge_call(x2d, za, at, ga, gb, w_up_a, w_up_b, w_out, g, wg, wu, wd, layer):
    t, d = x2d.shape
    d_rnn, d_att, d_ff = za.shape[1], at.shape[1], wg.shape[-1]
    row = lambda w: pl.BlockSpec((TOKEN_TILE, w), lambda i: (i, 0))
    return pl.pallas_call(
        _merge_kernel,
        out_shape=jax.ShapeDtypeStruct((t, d), F32),
        grid=(t // TOKEN_TILE,),
        in_specs=[row(d), row(d_rnn), row(d_att), row(d), row(d),
                  _layer_spec(layer, (d_rnn, d)), _layer_spec(layer, (d_att, d)),
                  _layer_spec(layer, (d, d)), _const_spec((1, d)), _layer_spec(layer, (d, d_ff)),
                  _layer_spec(layer, (d, d_ff)), _layer_spec(layer, (d_ff, d))],
        out_specs=row(d),
        compiler_params=pltpu.CompilerParams(
            dimension_semantics=("parallel",), vmem_limit_bytes=56 * MIB),
        name="merge",
    )(x2d, za, at, ga, gb, w_up_a, w_up_b, w_out, g, wg, wu, wd)


def _segment_permutation():
    seg = TIME_TILE // SUBLANES
    p = np.arange(TIME_TILE)
    perm = np.zeros((TIME_TILE, TIME_TILE), np.float32)
    perm[p, (p % SUBLANES) * seg + p // SUBLANES] = 1.0
    return perm


def _block_diag(w):
    depth, nb, n, _ = w.shape
    tiled = jnp.tile(w.reshape(depth, nb * n, n), (1, 1, nb))
    blk = np.arange(nb * n) // n
    return jnp.where(jnp.asarray(blk[:, None] == blk[None, :]), tiled, 0)


def kernel(x, norm_ffn1, ffn1_w_gate, ffn1_w_up, ffn1_w_down, norm_mix, w_in, gate_bias,
           conv_w, conv_b, rg_w_a, rg_b_a, rg_w_x, rg_b_x, rg_lambda, w_up_a,
           q_gain, k_gain, rel_table, w_up_b, w_out,
           norm_ffn2, ffn2_w_gate, ffn2_w_up, ffn2_w_down):
    bsz, s, d = x.shape
    depth = w_in.shape[0]
    d_rnn = conv_w.shape[-1]
    d_att = w_up_b.shape[1]
    assert q_gain.shape[-1] == HEAD_DIM and rel_table.shape[-1] == 2 * REL_CLIP + 1
    assert (bsz * s) % TOKEN_TILE == 0 and s % TIME_TILE == 0 and s % (ATT_SUB * ATT_QB) == 0
    assert s % TOKEN_TILE == 0 and d_att == d and TOKEN_TILE % TIME_TILE == 0

    mask = jnp.asarray(_attn_valid_mask())
    perm = _segment_permutation()
    perm, unperm = jnp.asarray(perm, BF16), jnp.asarray(perm.T, BF16)
    row = lambda a: a.reshape(1, -1).astype(F32)
    bf = lambda a: a.astype(BF16)
    rows8 = lambda a: jnp.broadcast_to(a.astype(F32), a.shape[:-2] + (SUBLANES, a.shape[-1]))
    q_scale = HEAD_DIM ** -0.5 * LOG2E

    w_in_b = bf(w_in)
    bias = _bias_call(rel_table.reshape(-1, rel_table.shape[-1]))
    w_ax = bf(jnp.concatenate([_block_diag(rg_w_a), _block_diag(rg_w_x)], axis=2))
    ffn1 = (bf(ffn1_w_gate), bf(ffn1_w_up), bf(ffn1_w_down))
    ffn2 = (bf(ffn2_w_gate), bf(ffn2_w_up), bf(ffn2_w_down))
    w_up_a_b, w_up_b_b, w_out_b = bf(w_up_a), bf(w_up_b), bf(w_out)

    x2d = x.reshape(bsz * s, d)
    for l in range(depth):
        x2d = _ffn_call(x2d, row(norm_ffn1[l]), *ffn1, l)
        q_gain_t = jnp.broadcast_to((q_gain[l].astype(F32) * q_scale)[:, None],
                                    (HEAD_DIM, TOKEN_TILE))
        k_gain2 = jnp.tile(row(k_gain[l]), (1, LANES // HEAD_DIM))
        xr, yr, k, ga, gb, qt, vt = _proj_call(
            x2d, row(norm_mix[l]), w_in_b, row(gate_bias[l]), q_gain_t, k_gain2,
            perm, d_rnn, d_att, l)
        za = _rnn_call(xr.reshape(bsz, s, d_rnn), yr.reshape(bsz, s, d_rnn),
                       rows8(conv_w[l][:, None, :]), rows8(conv_b[l][None, :]), w_ax, row(rg_b_a[l]),
                       row(rg_b_x[l]), row(rg_lambda[l]), unperm, rg_w_a.shape[1], l)
        at = _attn_call(qt.reshape(bsz, s // ATT_QB, d_att, ATT_QB), k.reshape(bsz, s, d_att),
                        vt.reshape(bsz, s // ATT_QB, d_att, ATT_QB),
                        bias, mask, l)
        x2d = _merge_call(x2d, za.reshape(bsz * s, d_rnn), at.reshape(bsz * s, d_att), ga, gb,
                          w_up_a_b, w_up_b_b, w_out_b, row(norm_ffn2[l]), *ffn2, l)
    return x2d.reshape(bsz, s, d)
```

```python
import functools

import numpy as np
import jax
import jax.numpy as jnp
from jax import lax
from jax.experimental import pallas as pl
from jax.experimental.pallas import tpu as pltpu

F32 = jnp.float32
BF16 = jnp.bfloat16

LANES = 128
SUBLANES = 8
MXU_DIM = 256
MIB = 1024 * 1024

CHUNK = 64
LEFT_CHUNKS = 8
HEAD_DIM = 64
REL_CLIP = 128
CONV_WIDTH = 4
LRU_C = 8.0
EPS = 1e-6
LOG2E = 1.4426950408889634

TOKEN_TILE = 512
FF_TILE = 3 * MXU_DIM
TIME_TILE = 256
ATT_QB = MXU_DIM
ATT_NBLK = LEFT_CHUNKS * CHUNK // ATT_QB + 1
ATT_SUB = 8
ATT_HG = MXU_DIM // HEAD_DIM
assert (LEFT_CHUNKS * CHUNK) % ATT_QB == 0 and ATT_QB % CHUNK == 0 and TOKEN_TILE % ATT_QB == 0


def _const_spec(shape):
    nd = len(shape)
    return pl.BlockSpec(shape, lambda *_: (0,) * nd, pipeline_mode=pl.Buffered(1))


def _layer_spec(layer, shape):
    nd = len(shape)
    return pl.BlockSpec((None,) + tuple(shape), lambda *_: (layer,) + (0,) * nd,
                        pipeline_mode=pl.Buffered(1))


def _rms_norm(x, g):
    ms = jnp.mean(x * x, axis=-1, keepdims=True)
    return x * lax.rsqrt(ms + EPS) * g


def _dot(a, b):
    return jnp.dot(a, b, preferred_element_type=F32)


def _dot_tt(a, b):
    return lax.dot_general(a, b, (((0,), (1,)), ((), ())), preferred_element_type=F32)


def _ffn_residual(x, g_ref, wg_ref, wu_ref, wd_ref):
    d_ff = wg_ref.shape[1]
    h = _rms_norm(x, g_ref[...]).astype(BF16)
    acc = None
    for c0 in range(0, d_ff, FF_TILE):
        c1 = min(c0 + FF_TILE, d_ff)
        gate = _dot(h, wg_ref[:, c0:c1])
        up = _dot(h, wu_ref[:, c0:c1])
        act = (gate * jax.nn.sigmoid(gate) * up).astype(BF16)
        part = _dot(act, wd_ref[c0:c1, :])
        acc = part if acc is None else acc + part
    return x + 0.5 * acc


def _ffn_kernel(x_ref, g_ref, wg_ref, wu_ref, wd_ref, o_ref):
    o_ref[...] = _ffn_residual(x_ref[...], g_ref, wg_ref, wu_ref, wd_ref)


def _ffn_call(x2d, g, wg, wu, wd, layer):
    t, d = x2d.shape
    d_ff = wg.shape[-1]
    row = pl.BlockSpec((TOKEN_TILE, d), lambda i: (i, 0))
    return pl.pallas_call(
        _ffn_kernel,
        out_shape=jax.ShapeDtypeStruct((t, d), F32),
        grid=(t // TOKEN_TILE,),
        in_specs=[row, _const_spec((1, d)), _layer_spec(layer, (d, d_ff)),
                  _layer_spec(layer, (d, d_ff)), _layer_spec(layer, (d_ff, d))],
        out_specs=row,
        compiler_params=pltpu.CompilerParams(
            dimension_semantics=("parallel",), vmem_limit_bytes=48 * MIB),
        name="ffn",
    )(x2d, g, wg, wu, wd)


def _head_rms_norm(x, gain2):
    rows, width = x.shape
    low = lax.broadcasted_iota(jnp.int32, (rows, LANES), 1) < HEAD_DIM
    out = []
    for c in range(0, width, LANES):
        xs = x[:, c:c + LANES]
        sq = xs * xs
        ms_lo = jnp.sum(jnp.where(low, sq, 0.0), axis=-1, keepdims=True) * (1.0 / HEAD_DIM)
        ms_hi = jnp.sum(jnp.where(low, 0.0, sq), axis=-1, keepdims=True) * (1.0 / HEAD_DIM)
        r = jnp.where(low, lax.rsqrt(ms_lo + EPS), lax.rsqrt(ms_hi + EPS))
        out.append(xs * r * gain2)
    return jnp.concatenate(out, axis=-1)


def _head_rms_norm_t(xt, gain_t):
    out = []
    for r0 in range(0, xt.shape[0], HEAD_DIM):
        xs = xt[r0:r0 + HEAD_DIM, :]
        ms = jnp.mean(xs * xs, axis=0, keepdims=True)
        out.append(xs * lax.rsqrt(ms + EPS) * gain_t)
    return jnp.concatenate(out, axis=0)


def _rnn_tile_steps(n_blocks, xy_ref, r0, d_rnn, cw_ref, cb_ref, wax_ref, ba_ref, bx_ref, lam_ref,
                    unperm_ref, za_ref, tail_ref, h_ref, u_ref, a_ref, b_ref, z_ref):
    rows, c = TIME_TILE, d_rnn
    seg = rows // SUBLANES
    taps = CONV_WIDTH - 1
    pieces = 4
    sub = lax.broadcasted_iota(jnp.int32, (SUBLANES, c), 0)

    def group(col0, j):
        return xy_ref[r0 + j * SUBLANES:r0 + (j + 1) * SUBLANES, col0:col0 + c]

    last = [group(0, seg - k) for k in range(1, taps + 1)]
    hist = [jnp.where(sub == 0, pltpu.roll(tail_ref[k], 1, axis=0), pltpu.roll(last[k], 1, axis=0))
            for k in range(taps - 1, -1, -1)]
    for k in range(taps):
        tail_ref[k] = last[k]
    for j in range(seg):
        hist.append(last[seg - 1 - j] if seg - 1 - j < taps else group(0, j))
        u = cb_ref[...]
        for t in range(CONV_WIDTH):
            u = u + hist[t] * cw_ref[t]
        u_ref[j * SUBLANES:(j + 1) * SUBLANES, :] = u
        hist.pop(0)
        if (j + 1) % (seg // pieces) == 0:
            yield

    n = c // n_blocks
    decay_rate = (-LRU_C) * jax.nn.softplus(-lam_ref[...])
    for c0 in range(0, c, MXU_DIM):
        c1 = min(c0 + MXU_DIM, c)
        w0 = (c0 // n) * n // LANES * LANES
        w1 = min(-(-(-(-c1 // n) * n) // LANES) * LANES, c)
        u_bf = u_ref[:, w0:w1].astype(BF16)
        r = jax.nn.sigmoid(_dot(u_bf, wax_ref[w0:w1, c0:c1]) + ba_ref[:, c0:c1])
        i = jax.nn.sigmoid(_dot(u_bf, wax_ref[w0:w1, c + c0:c + c1]) + bx_ref[:, c0:c1])
        log_a = r * decay_rate[:, c0:c1]
        a = jnp.exp(log_a)
        a_ref[:, c0:c1] = a
        b_ref[:, c0:c1] = jnp.sqrt(-jnp.tanh(log_a) * (a * a + 1.0)) * (i * u_ref[:, c0:c1])
        yield

    state = jnp.zeros((SUBLANES, c), F32)
    decay = jnp.ones((SUBLANES, c), F32)
    for j in range(seg):
        sl = slice(j * SUBLANES, (j + 1) * SUBLANES)
        a_j = a_ref[sl, :]
        state = a_j * state + b_ref[sl, :]
        decay = a_j * decay
        b_ref[sl, :] = state
        a_ref[sl, :] = decay
    yield

    d = 1
    while d < SUBLANES:
        keep = sub >= d
        decay_prev = jnp.where(keep, pltpu.roll(decay, d, axis=0), 1.0)
        state_prev = jnp.where(keep, pltpu.roll(state, d, axis=0), 0.0)
        state = decay * state_prev + state
        decay = decay * decay_prev
        d *= 2
    ends = decay * h_ref[...] + state
    carry = jnp.where(sub == 0, h_ref[...], pltpu.roll(ends, 1, axis=0))
    h_ref[...] = ends[SUBLANES - 1:SUBLANES, :]

    for j in range(seg):
        sl = slice(j * SUBLANES, (j + 1) * SUBLANES)
        z_ref[sl, :] = jax.nn.gelu(group(c, j)) * (b_ref[sl, :] + a_ref[sl, :] * carry)
        if (j + 1) % (seg // pieces) == 0:
            yield

    za_ref[r0:r0 + rows, :] = _dot(unperm_ref[...], z_ref[...].astype(BF16)).astype(BF16)
    yield


def _interleave(primary, secondary, ratio):
    live = True
    while live:
        live = next(primary, "done") != "done"
        for _ in range(ratio):
            live = (next(secondary, "done") != "done") or live


def _mix_kernel(d_rnn, d_att, n_blocks, steps_per_seq,
                x_ref, g_ref, w_ref, qg_ref, kg_ref, perm_ref, unperm_ref,
                cw_ref, cb_ref, wax_ref, ba_ref, bx_ref, lam_ref,
                k_ref, qt_ref, vt_ref, za_ref,
                xy_ref, tail_ref, h_ref, u_ref, a_ref, b_ref, z_ref):
    taps = CONV_WIDTH - 1
    rows = x_ref.shape[0]
    c_q = 2 * d_rnn

    @pl.when(pl.program_id(0) % steps_per_seq == 0)
    def _():
        tail_ref[...] = jnp.zeros((taps, SUBLANES, d_rnn), F32)
        h_ref[...] = jnp.zeros((1, d_rnn), F32)

    h = _rms_norm(x_ref[...], g_ref[...]).astype(BF16)

    def recurrent_inputs(r0):
        hp = _dot(perm_ref[...], h[r0:r0 + TIME_TILE, :]).astype(BF16)
        for c0 in range(0, c_q, FF_TILE):
            c1 = min(c0 + FF_TILE, c_q)
            xy_ref[r0:r0 + TIME_TILE, c0:c1] = _dot(hp, w_ref[:, c0:c1])
            yield

    def projections():
        for r0 in range(TIME_TILE, rows, TIME_TILE):
            yield from recurrent_inputs(r0)
        for c0 in range(0, d_att, MXU_DIM):
            col = c_q + d_att + c0
            k = _dot(h, w_ref[:, col:col + MXU_DIM])
            k_ref[:, c0:c0 + MXU_DIM] = _head_rms_norm(k, kg_ref[...]).astype(BF16)
            yield
        for src0, out_ref, norm in ((c_q, qt_ref, True), (c_q + 2 * d_att, vt_ref, False)):
            for c0 in range(0, d_att, MXU_DIM):
                t = _dot_tt(w_ref[:, src0 + c0:src0 + c0 + MXU_DIM], h)
                if norm:
                    t = _head_rms_norm_t(t, qg_ref[...])
                t = t.astype(BF16)
                for n in range(out_ref.shape[0]):
                    out_ref[n, c0:c0 + MXU_DIM, :] = t[:, n * ATT_QB:(n + 1) * ATT_QB]
                yield

    def recurrence():
        for r0 in range(0, rows, TIME_TILE):
            yield from _rnn_tile_steps(
                n_blocks, xy_ref, r0, d_rnn, cw_ref, cb_ref, wax_ref, ba_ref, bx_ref, lam_ref,
                unperm_ref, za_ref, tail_ref, h_ref, u_ref, a_ref, b_ref, z_ref)

    for _ in recurrent_inputs(0):
        pass
    _interleave(projections(), recurrence(), 2)


def _mix_call(x2d, g, w_in, q_gain_t, k_gain2, perm, unperm, conv_w, conv_b, w_ax, b_a, b_x, lam,
              seq_len, d_rnn, d_att, n_blocks, layer):
    t, d = x2d.shape
    nt = TOKEN_TILE // ATT_QB
    c = d_rnn
    row = lambda w: pl.BlockSpec((TOKEN_TILE, w), lambda i: (i, 0))
    sds = lambda w, dt: jax.ShapeDtypeStruct((t, w), dt)
    tiles = pl.BlockSpec((nt, d_att, ATT_QB), lambda i: (i, 0, 0))
    tiles_sds = jax.ShapeDtypeStruct((t // ATT_QB, d_att, ATT_QB), BF16)
    tile = pltpu.VMEM((TIME_TILE, c), F32)
    return pl.pallas_call(
        functools.partial(_mix_kernel, d_rnn, d_att, n_blocks, seq_len // TOKEN_TILE),
        out_shape=(sds(d_att, BF16), tiles_sds, tiles_sds, sds(d_rnn, BF16)),
        grid=(t // TOKEN_TILE,),
        in_specs=[row(d), _const_spec((1, d)), _layer_spec(layer, (d, 2 * d_rnn + 3 * d_att)),
                  _const_spec((HEAD_DIM, TOKEN_TILE)), _const_spec((1, LANES)),
                  _const_spec(perm.shape), _const_spec(unperm.shape),
                  _const_spec((CONV_WIDTH, SUBLANES, c)), _const_spec((SUBLANES, c)),
                  _layer_spec(layer, (c, 2 * c)), _const_spec((1, c)), _const_spec((1, c)),
                  _const_spec((1, c))],
        out_specs=(row(d_att), tiles, tiles, row(d_rnn)),
        scratch_shapes=[pltpu.VMEM((TOKEN_TILE, 2 * c), F32),
                        pltpu.VMEM((CONV_WIDTH - 1, SUBLANES, c), F32),
                        pltpu.VMEM((1, c), F32), tile, tile, tile, tile],
        compiler_params=pltpu.CompilerParams(
            dimension_semantics=("arbitrary",), vmem_limit_bytes=56 * MIB),
        name="mix",
    )(x2d, g, w_in, q_gain_t, k_gain2, perm, unperm, conv_w, conv_b, w_ax, b_a, b_x, lam)


def _bias_kernel(base_ref, o_ref):
    for h in range(o_ref.shape[0]):
        for n in range(ATT_NBLK):
            if ATT_QB * n - (ATT_QB - 1) >= REL_CLIP:
                o_ref[h, n] = jnp.broadcast_to(base_ref[h, n][:, :ATT_QB], (ATT_QB, ATT_QB))
                continue
            x = jnp.broadcast_to(base_ref[h, n], (ATT_QB, 2 * ATT_QB))
            o_ref[h, n] = pltpu.roll(x, 0, 1, stride=1, stride_axis=0)[:, :ATT_QB]


def _bias_call(rel_table):
    nh = rel_table.shape[0]
    j = np.arange(2 * ATT_QB)
    rel = np.where(j < ATT_QB, j, j - 2 * ATT_QB)[None, :] + ATT_QB * np.arange(ATT_NBLK)[:, None]
    idx = np.clip(rel, -REL_CLIP, REL_CLIP) + REL_CLIP
    base = (rel_table.astype(F32) * LOG2E)[:, idx].reshape(nh, ATT_NBLK, 1, 2 * ATT_QB)
    return pl.pallas_call(
        _bias_kernel,
        out_shape=jax.ShapeDtypeStruct((nh, ATT_NBLK, ATT_QB, ATT_QB), F32),
        grid=(nh // ATT_HG,),
        in_specs=[pl.BlockSpec((ATT_HG, ATT_NBLK, 1, 2 * ATT_QB), lambda h: (h, 0, 0, 0))],
        out_specs=pl.BlockSpec((ATT_HG, ATT_NBLK, ATT_QB, ATT_QB), lambda h: (h, 0, 0, 0)),
        compiler_params=pltpu.CompilerParams(dimension_semantics=("parallel",)),
        name="bias",
    )(base)


def _slab_state(n, kc, half):
    per = ATT_QB // CHUNK
    ok = [0 <= per * n + qc - kc <= LEFT_CHUNKS
          for qc in range(half * (LANES // CHUNK), (half + 1) * (LANES // CHUNK))]
    return {(True, True): "all", (False, True): "hi", (True, False): "lo",
            (False, False): "none"}[tuple(ok)]


def _head_query(qt, h):
    lanes = ATT_HG * HEAD_DIM
    r0, r1 = h * HEAD_DIM, (h + 1) * HEAD_DIM
    pieces = []
    if r0:
        pieces.append(jnp.zeros((r0, ATT_QB), BF16))
    pieces.append(qt[r0:r1, :])
    if r1 < lanes:
        pieces.append(jnp.zeros((lanes - r1, ATT_QB), BF16))
    return jnp.concatenate(pieces, axis=0)


def _value_rows(vt_ref, tile, h):
    vt = vt_ref[0, tile, h * HEAD_DIM:(h + 1) * HEAD_DIM, :]
    return jnp.concatenate([vt, jnp.ones((2 * SUBLANES, ATT_QB), BF16)], axis=0)


def _attn_kernel(qt_ref, k_ref, vt_ref, bias_ref, mask_ref, o_ref):
    step = pl.program_id(2)
    halves = ATT_QB // LANES
    kchunks = ATT_QB // CHUNK

    def load_keys(tiles):
        starts = [t * ATT_QB if isinstance(t, int) else pl.multiple_of(t * ATT_QB, ATT_QB)
                  for t in tiles]
        return [k_ref[0, pl.ds(start, ATT_QB), :] for start in starts]

    def head_scores(j, tiles, h):
        qm = _head_query(qt_ref[0, j], h)
        return [_dot(kb, qm) for kb in load_keys(tiles)]

    def column_max(x):
        m = []
        for a in range(halves):
            top = functools.reduce(jnp.maximum, [v for (_, _, aa), v in x.items() if aa == a])
            m.append(jnp.max(top, axis=0, keepdims=True))
        return m

    def interior_scores(j, tiles, h):
        s = head_scores(j, tiles, h)
        hi = lax.broadcasted_iota(jnp.int32, (CHUNK, LANES), 1) >= CHUNK
        x = {}
        for n in range(ATT_NBLK):
            for kc in range(kchunks):
                for a in range(halves):
                    state = _slab_state(n, kc, a)
                    if state == "none":
                        continue
                    rs = slice(kc * CHUNK, (kc + 1) * CHUNK)
                    cs = slice(a * LANES, (a + 1) * LANES)
                    v = s[n][rs, cs] + bias_ref[h, n, rs, cs]
                    if state == "hi":
                        v = jnp.where(hi, v, -1e30)
                    elif state == "lo":
                        v = jnp.where(hi, -1e30, v)
                    x[n, kc, a] = v
        return x, column_max(x)

    def edge_scores(j, gi, tiles, h):
        s = head_scores(j, tiles, h)
        x = {}
        for n in range(ATT_NBLK):
            valid = mask_ref[n] > (0.5 if gi >= n else 2.0)
            xn = jnp.where(valid, s[n] + bias_ref[h, n], -1e30)
            for kc in range(kchunks):
                for a in range(halves):
                    x[n, kc, a] = xn[kc * CHUNK:(kc + 1) * CHUNK, a * LANES:(a + 1) * LANES]
        return x, column_max(x)

    def softmax_pv(tiles, h, x, m):
        o_ext = None
        for n in range(ATT_NBLK):
            rows = []
            for kc in range(kchunks):
                rows.append(jnp.concatenate(
                    [jnp.exp2((x[n, kc, a] - m[a]).astype(BF16)) if (n, kc, a) in x
                     else jnp.zeros((CHUNK, LANES), BF16) for a in range(halves)], axis=1))
            part = _dot(_value_rows(vt_ref, tiles[n], h), jnp.concatenate(rows, axis=0))
            o_ext = part if o_ext is None else o_ext + part
        return o_ext[:HEAD_DIM] * (1.0 / o_ext[HEAD_DIM:HEAD_DIM + 1])

    def run_units(first):
        base = 0 if first else step * ATT_SUB
        tiles = [[(max(j - n, 0) if first else base + j - n) for n in range(ATT_NBLK)]
                 for j in range(ATT_SUB)]

        def scores(j, h):
            if first and j < ATT_NBLK - 1:
                return edge_scores(j, j, tiles[j], h)
            return interior_scores(j, tiles[j], h)

        pending = [scores(0, h) for h in range(ATT_HG)]
        for j in range(ATT_SUB):
            outs = []
            for h in range(ATT_HG):
                x, m = pending[h]
                if j + 1 < ATT_SUB:
                    pending[h] = scores(j + 1, h)
                outs.append(softmax_pv(tiles[j], h, x, m))
            ot = jnp.concatenate(outs, axis=0)
            o_ref[0, j * ATT_QB:(j + 1) * ATT_QB, :] = ot.T.astype(BF16)

    @pl.when(step == 0)
    def _():
        run_units(True)

    @pl.when(step > 0)
    def _():
        run_units(False)


def _attn_call(qt, k, vt, bias, mask, layer):
    bsz, s, d_att = k.shape
    lanes = ATT_HG * HEAD_DIM
    rows = ATT_SUB * ATT_QB
    nt = s // ATT_QB
    return pl.pallas_call(
        _attn_kernel,
        out_shape=jax.ShapeDtypeStruct((bsz, s, d_att), BF16),
        grid=(bsz, d_att // lanes, s // rows),
        in_specs=[pl.BlockSpec((1, ATT_SUB, lanes, ATT_QB), lambda b, g, i: (b, i, g, 0)),
                  pl.BlockSpec((1, s, lanes), lambda b, g, i: (b, 0, g)),
                  pl.BlockSpec((1, nt, lanes, ATT_QB), lambda b, g, i: (b, 0, g, 0)),
                  pl.BlockSpec((ATT_HG, ATT_NBLK, ATT_QB, ATT_QB),
                               lambda b, g, i: (layer * (d_att // lanes) + g, 0, 0, 0)),
                  _const_spec((ATT_NBLK, ATT_QB, ATT_QB))],
        out_specs=pl.BlockSpec((1, rows, lanes), lambda b, g, i: (b, i, g)),
        compiler_params=pltpu.CompilerParams(
            dimension_semantics=("parallel", "parallel", "arbitrary"),
            vmem_limit_bytes=48 * MIB),
        name="attn",
    )(qt, k, vt, bias, mask)


def _attn_valid_mask():
    per = ATT_QB // CHUNK
    n = np.arange(ATT_NBLK)[:, None, None]
    kc = (np.arange(ATT_QB) // CHUNK)[None, :, None]
    qc = (np.arange(ATT_QB) // CHUNK)[None, None, :]
    dist = per * n + qc - kc
    return ((dist >= 0) & (dist <= LEFT_CHUNKS)).astype(np.float32)


def _merge_kernel(x_ref, za_ref, at_ref, gm_ref, wgate_ref, gbias_ref, wa_ref, wb_ref, wo_ref,
                  g_ref, wg_ref, wu_ref, wd_ref, o_ref):
    d = x_ref.shape[1]
    x = x_ref[...]
    gates = _dot(_rms_norm(x, gm_ref[...]).astype(BF16), wgate_ref[...]) + gbias_ref[...]
    y_a = _dot(za_ref[...], wa_ref[...])
    y_b = _dot(at_ref[...], wb_ref[...])
    mix = (jax.nn.sigmoid(gates[:, :d]) * y_a + jax.nn.sigmoid(gates[:, d:]) * y_b).astype(BF16)
    x = x + _dot(mix, wo_ref[...])
    o_ref[...] = _ffn_residual(x, g_ref, wg_ref, wu_ref, wd_ref)


def _merge_call(x2d, za, at, g_mix, w_gate, gate_bias, w_up_a, w_up_b, w_out, g, wg, wu, wd, layer):
    t, d = x2d.shape
    d_rnn, d_att, d_ff = za.shape[1], at.shape[1], wg.shape[-1]
    row = lambda w: pl.BlockSpec((TOKEN_TILE, w), lambda i: (i, 0))
    return pl.pallas_call(
        _merge_kernel,
        out_shape=jax.ShapeDtypeStruct((t, d), F32),
        grid=(t // TOKEN_TILE,),
        in_specs=[row(d), row(d_rnn), row(d_att), _const_spec((1, d)),
                  _layer_spec(layer, (d, 2 * d)), _const_spec((1, 2 * d)),
                  _layer_spec(layer, (d_rnn, d)), _layer_spec(layer, (d_att, d)),
                  _layer_spec(layer, (d, d)), _const_spec((1, d)), _layer_spec(layer, (d, d_ff)),
                  _layer_spec(layer, (d, d_ff)), _layer_spec(layer, (d_ff, d))],
        out_specs=row(d),
        compiler_params=pltpu.CompilerParams(
            dimension_semantics=("parallel",), vmem_limit_bytes=56 * MIB),
        name="merge",
    )(x2d, za, at, g_mix, w_gate, gate_bias, w_up_a, w_up_b, w_out, g, wg, wu, wd)


def _segment_permutation():
    seg = TIME_TILE // SUBLANES
    p = np.arange(TIME_TILE)
    perm = np.zeros((TIME_TILE, TIME_TILE), np.float32)
    perm[p, (p % SUBLANES) * seg + p // SUBLANES] = 1.0
    return perm


def _block_diag(w):
    depth, nb, n, _ = w.shape
    tiled = jnp.tile(w.reshape(depth, nb * n, n), (1, 1, nb))
    blk = np.arange(nb * n) // n
    return jnp.where(jnp.asarray(blk[:, None] == blk[None, :]), tiled, 0)


def kernel(x, norm_ffn1, ffn1_w_gate, ffn1_w_up, ffn1_w_down, norm_mix, w_in, gate_bias,
           conv_w, conv_b, rg_w_a, rg_b_a, rg_w_x, rg_b_x, rg_lambda, w_up_a,
           q_gain, k_gain, rel_table, w_up_b, w_out,
           norm_ffn2, ffn2_w_gate, ffn2_w_up, ffn2_w_down):
    bsz, s, d = x.shape
    depth = w_in.shape[0]
    d_rnn = conv_w.shape[-1]
    d_att = w_up_b.shape[1]
    assert q_gain.shape[-1] == HEAD_DIM and rel_table.shape[-1] == 2 * REL_CLIP + 1
    assert (bsz * s) % TOKEN_TILE == 0 and s % TIME_TILE == 0 and s % (ATT_SUB * ATT_QB) == 0
    assert s % TOKEN_TILE == 0 and d_att == d and TOKEN_TILE % TIME_TILE == 0

    mask = jnp.asarray(_attn_valid_mask())
    perm = _segment_permutation()
    perm, unperm = jnp.asarray(perm, BF16), jnp.asarray(perm.T, BF16)
    row = lambda a: a.reshape(1, -1).astype(F32)
    bf = lambda a: a.astype(BF16)
    rows8 = lambda a: jnp.broadcast_to(a.astype(F32), a.shape[:-2] + (SUBLANES, a.shape[-1]))
    q_scale = HEAD_DIM ** -0.5 * LOG2E

    w_in_b = bf(w_in)
    w_gate = w_in_b[:, :, 2 * d_rnn + 3 * d_att:]
    bias = _bias_call(rel_table.reshape(-1, rel_table.shape[-1]))
    w_ax = bf(jnp.concatenate([_block_diag(rg_w_a), _block_diag(rg_w_x)], axis=2))
    ffn1 = (bf(ffn1_w_gate), bf(ffn1_w_up), bf(ffn1_w_down))
    ffn2 = (bf(ffn2_w_gate), bf(ffn2_w_up), bf(ffn2_w_down))
    w_up_a_b, w_up_b_b, w_out_b = bf(w_up_a), bf(w_up_b), bf(w_out)

    x2d = x.reshape(bsz * s, d)
    for l in range(depth):
        x2d = _ffn_call(x2d, row(norm_ffn1[l]), *ffn1, l)
        q_gain_t = jnp.broadcast_to((q_gain[l].astype(F32) * q_scale)[:, None],
                                    (HEAD_DIM, TOKEN_TILE))
        k_gain2 = jnp.tile(row(k_gain[l]), (1, LANES // HEAD_DIM))
        k, qt, vt, za = _mix_call(
            x2d, row(norm_mix[l]), w_in_b, q_gain_t, k_gain2, perm, unperm,
            rows8(conv_w[l][:, None, :]), rows8(conv_b[l][None, :]), w_ax, row(rg_b_a[l]),
            row(rg_b_x[l]), row(rg_lambda[l]), s, d_rnn, d_att, rg_w_a.shape[1], l)
        at = _attn_call(qt.reshape(bsz, s // ATT_QB, d_att, ATT_QB), k.reshape(bsz, s, d_att),
                        vt.reshape(bsz, s // ATT_QB, d_att, ATT_QB),
                        bias, mask, l)
        x2d = _merge_call(x2d, za, at.reshape(bsz * s, d_att), row(norm_mix[l]), w_gate,
                          row(gate_bias[l]), w_up_a_b, w_up_b_b, w_out_b, row(norm_ffn2[l]),
                          *ffn2, l)
    return x2d.reshape(bsz, s, d)
```

```python
import functools

import numpy as np
import jax
import jax.numpy as jnp
from jax import lax
from jax.experimental import pallas as pl
from jax.experimental.pallas import tpu as pltpu

F32 = jnp.float32
BF16 = jnp.bfloat16

LANES = 128
SUBLANES = 8
MXU_DIM = 256
MIB = 1024 * 1024

CHUNK = 64
LEFT_CHUNKS = 8
HEAD_DIM = 64
REL_CLIP = 128
CONV_WIDTH = 4
LRU_C = 8.0
EPS = 1e-6
LOG2E = 1.4426950408889634

TOKEN_TILE = 512
FF_TILE = 3 * MXU_DIM
TIME_TILE = 256
ATT_QB = MXU_DIM
ATT_NBLK = LEFT_CHUNKS * CHUNK // ATT_QB + 1
ATT_SUB = 8
ATT_HG = MXU_DIM // HEAD_DIM
assert (LEFT_CHUNKS * CHUNK) % ATT_QB == 0 and ATT_QB % CHUNK == 0 and TOKEN_TILE % ATT_QB == 0


def _const_spec(shape):
    nd = len(shape)
    return pl.BlockSpec(shape, lambda *_: (0,) * nd, pipeline_mode=pl.Buffered(1))


def _layer_spec(layer, shape):
    nd = len(shape)
    return pl.BlockSpec((None,) + tuple(shape), lambda *_: (layer,) + (0,) * nd,
                        pipeline_mode=pl.Buffered(1))


def _rms_norm(x, g):
    ms = jnp.mean(x * x, axis=-1, keepdims=True)
    return x * lax.rsqrt(ms + EPS) * g


def _dot(a, b):
    return jnp.dot(a, b, preferred_element_type=F32)


def _dot_tt(a, b):
    return lax.dot_general(a, b, (((0,), (1,)), ((), ())), preferred_element_type=F32)


def _ffn_residual(x, g_ref, wg_ref, wu_ref, wd_ref):
    d_ff = wg_ref.shape[1]
    h = _rms_norm(x, g_ref[...]).astype(BF16)
    acc = None
    for c0 in range(0, d_ff, FF_TILE):
        c1 = min(c0 + FF_TILE, d_ff)
        gate = _dot(h, wg_ref[:, c0:c1])
        up = _dot(h, wu_ref[:, c0:c1])
        act = (gate * jax.nn.sigmoid(gate) * up).astype(BF16)
        part = _dot(act, wd_ref[c0:c1, :])
        acc = part if acc is None else acc + part
    return x + 0.5 * acc


def _ffn_kernel(x_ref, g_ref, wg_ref, wu_ref, wd_ref, o_ref):
    o_ref[...] = _ffn_residual(x_ref[...], g_ref, wg_ref, wu_ref, wd_ref)


def _ffn_call(x2d, g, wg, wu, wd, layer):
    t, d = x2d.shape
    d_ff = wg.shape[-1]
    row = pl.BlockSpec((TOKEN_TILE, d), lambda i: (i, 0))
    return pl.pallas_call(
        _ffn_kernel,
        out_shape=jax.ShapeDtypeStruct((t, d), F32),
        grid=(t // TOKEN_TILE,),
        in_specs=[row, _const_spec((1, d)), _layer_spec(layer, (d, d_ff)),
                  _layer_spec(layer, (d, d_ff)), _layer_spec(layer, (d_ff, d))],
        out_specs=row,
        compiler_params=pltpu.CompilerParams(
            dimension_semantics=("parallel",), vmem_limit_bytes=48 * MIB),
        name="ffn",
    )(x2d, g, wg, wu, wd)


def _head_rms_norm(x, gain2):
    rows, width = x.shape
    low = lax.broadcasted_iota(jnp.int32, (rows, LANES), 1) < HEAD_DIM
    out = []
    for c in range(0, width, LANES):
        xs = x[:, c:c + LANES]
        sq = xs * xs
        ms_lo = jnp.sum(jnp.where(low, sq, 0.0), axis=-1, keepdims=True) * (1.0 / HEAD_DIM)
        ms_hi = jnp.sum(jnp.where(low, 0.0, sq), axis=-1, keepdims=True) * (1.0 / HEAD_DIM)
        r = jnp.where(low, lax.rsqrt(ms_lo + EPS), lax.rsqrt(ms_hi + EPS))
        out.append(xs * r * gain2)
    return jnp.concatenate(out, axis=-1)


def _head_rms_norm_t(xt, gain_t):
    out = []
    for r0 in range(0, xt.shape[0], HEAD_DIM):
        xs = xt[r0:r0 + HEAD_DIM, :]
        ms = jnp.mean(xs * xs, axis=0, keepdims=True)
        out.append(xs * lax.rsqrt(ms + EPS) * gain_t)
    return jnp.concatenate(out, axis=0)


def _rnn_tile_steps(n_blocks, xy_ref, r0, d_rnn, cw_ref, cb_ref, wax_ref, ba_ref, bx_ref, lam_ref,
                    unperm_ref, za_ref, tail_ref, h_ref, u_ref, a_ref, b_ref, z_ref):
    rows, c = TIME_TILE, d_rnn
    seg = rows // SUBLANES
    taps = CONV_WIDTH - 1
    pieces = 4
    sub = lax.broadcasted_iota(jnp.int32, (SUBLANES, c), 0)

    def group(col0, j):
        return xy_ref[r0 + j * SUBLANES:r0 + (j + 1) * SUBLANES, col0:col0 + c]

    last = [group(0, seg - k) for k in range(1, taps + 1)]
    hist = [jnp.where(sub == 0, pltpu.roll(tail_ref[k], 1, axis=0), pltpu.roll(last[k], 1, axis=0))
            for k in range(taps - 1, -1, -1)]
    for k in range(taps):
        tail_ref[k] = last[k]
    for j in range(seg):
        hist.append(last[seg - 1 - j] if seg - 1 - j < taps else group(0, j))
        u = cb_ref[...]
        for t in range(CONV_WIDTH):
            u = u + hist[t] * cw_ref[t]
        u_ref[j * SUBLANES:(j + 1) * SUBLANES, :] = u
        hist.pop(0)
        if (j + 1) % (seg // pieces) == 0:
            yield

    n = c // n_blocks
    decay_rate = (-LRU_C) * jax.nn.softplus(-lam_ref[...])
    for c0 in range(0, c, MXU_DIM):
        c1 = min(c0 + MXU_DIM, c)
        w0 = (c0 // n) * n // LANES * LANES
        w1 = min(-(-(-(-c1 // n) * n) // LANES) * LANES, c)
        u_bf = u_ref[:, w0:w1].astype(BF16)
        r = jax.nn.sigmoid(_dot(u_bf, wax_ref[w0:w1, c0:c1]) + ba_ref[:, c0:c1])
        i = jax.nn.sigmoid(_dot(u_bf, wax_ref[w0:w1, c + c0:c + c1]) + bx_ref[:, c0:c1])
        log_a = r * decay_rate[:, c0:c1]
        a = jnp.exp(log_a)
        a_ref[:, c0:c1] = a
        b_ref[:, c0:c1] = jnp.sqrt(-jnp.tanh(log_a) * (a * a + 1.0)) * (i * u_ref[:, c0:c1])
        yield

    state = jnp.zeros((SUBLANES, c), F32)
    decay = jnp.ones((SUBLANES, c), F32)
    for j in range(seg):
        sl = slice(j * SUBLANES, (j + 1) * SUBLANES)
        a_j = a_ref[sl, :]
        state = a_j * state + b_ref[sl, :]
        decay = a_j * decay
        b_ref[sl, :] = state
        a_ref[sl, :] = decay
    yield

    d = 1
    while d < SUBLANES:
        keep = sub >= d
        decay_prev = jnp.where(keep, pltpu.roll(decay, d, axis=0), 1.0)
        state_prev = jnp.where(keep, pltpu.roll(state, d, axis=0), 0.0)
        state = decay * state_prev + state
        decay = decay * decay_prev
        d *= 2
    ends = decay * h_ref[...] + state
    carry = jnp.where(sub == 0, h_ref[...], pltpu.roll(ends, 1, axis=0))
    h_ref[...] = ends[SUBLANES - 1:SUBLANES, :]

    for j in range(seg):
        sl = slice(j * SUBLANES, (j + 1) * SUBLANES)
        z_ref[sl, :] = jax.nn.gelu(group(c, j)) * (b_ref[sl, :] + a_ref[sl, :] * carry)
        if (j + 1) % (seg // pieces) == 0:
            yield

    za_ref[r0:r0 + rows, :] = _dot(unperm_ref[...], z_ref[...].astype(BF16)).astype(BF16)
    yield


def _interleave(primary, secondary, ratio):
    live = True
    while live:
        live = next(primary, "done") != "done"
        for _ in range(ratio):
            live = (next(secondary, "done") != "done") or live


def _mix_kernel(d_rnn, d_att, n_blocks, steps_per_seq,
                x_ref, g_ref, w_ref, qg_ref, kg_ref, perm_ref, unperm_ref,
                cw_ref, cb_ref, wax_ref, ba_ref, bx_ref, lam_ref,
                k_ref, qt_ref, vt_ref, za_ref,
                xy_ref, tail_ref, h_ref, u_ref, a_ref, b_ref, z_ref):
    taps = CONV_WIDTH - 1
    rows = x_ref.shape[0]
    c_q = 2 * d_rnn

    @pl.when(pl.program_id(0) % steps_per_seq == 0)
    def _():
        tail_ref[...] = jnp.zeros((taps, SUBLANES, d_rnn), F32)
        h_ref[...] = jnp.zeros((1, d_rnn), F32)

    h = _rms_norm(x_ref[...], g_ref[...]).astype(BF16)

    def recurrent_inputs(r0):
        hp = _dot(perm_ref[...], h[r0:r0 + TIME_TILE, :]).astype(BF16)
        for c0 in range(0, c_q, FF_TILE):
            c1 = min(c0 + FF_TILE, c_q)
            xy_ref[r0:r0 + TIME_TILE, c0:c1] = _dot(hp, w_ref[:, c0:c1])
            yield

    def later_inputs():
        for r0 in range(TIME_TILE, rows, TIME_TILE):
            yield from recurrent_inputs(r0)

    def keys():
        for c0 in range(0, d_att, MXU_DIM):
            col = c_q + d_att + c0
            k = _dot(h, w_ref[:, col:col + MXU_DIM])
            k_ref[:, c0:c0 + MXU_DIM] = _head_rms_norm(k, kg_ref[...]).astype(BF16)
            yield

    def transposed(src0, out_ref, norm):
        for c0 in range(0, d_att, MXU_DIM):
            t = _dot_tt(w_ref[:, src0 + c0:src0 + c0 + MXU_DIM], h)
            if norm:
                t = _head_rms_norm_t(t, qg_ref[...])
            t = t.astype(BF16)
            for n in range(out_ref.shape[0]):
                out_ref[n, c0:c0 + MXU_DIM, :] = t[:, n * ATT_QB:(n + 1) * ATT_QB]
            yield

    def projections():
        family = {"x": later_inputs(), "k": keys(), "q": transposed(c_q, qt_ref, True),
                  "v": transposed(c_q + 2 * d_att, vt_ref, False)}
        for name in "kkxxxxkkqqvvvvqq":
            next(family[name])
            yield
        assert all(next(g, "end") == "end" for g in family.values())
        assert rows == 2 * TIME_TILE

    def recurrence():
        for r0 in range(0, rows, TIME_TILE):
            yield from _rnn_tile_steps(
                n_blocks, xy_ref, r0, d_rnn, cw_ref, cb_ref, wax_ref, ba_ref, bx_ref, lam_ref,
                unperm_ref, za_ref, tail_ref, h_ref, u_ref, a_ref, b_ref, z_ref)

    for _ in recurrent_inputs(0):
        pass
    _interleave(projections(), recurrence(), 2)


def _mix_call(x2d, g, w_in, q_gain_t, k_gain2, perm, unperm, conv_w, conv_b, w_ax, b_a, b_x, lam,
              seq_len, d_rnn, d_att, n_blocks, layer):
    t, d = x2d.shape
    nt = TOKEN_TILE // ATT_QB
    c = d_rnn
    row = lambda w: pl.BlockSpec((TOKEN_TILE, w), lambda i: (i, 0))
    sds = lambda w, dt: jax.ShapeDtypeStruct((t, w), dt)
    tiles = pl.BlockSpec((nt, d_att, ATT_QB), lambda i: (i, 0, 0))
    tiles_sds = jax.ShapeDtypeStruct((t // ATT_QB, d_att, ATT_QB), BF16)
    tile = pltpu.VMEM((TIME_TILE, c), F32)
    return pl.pallas_call(
        functools.partial(_mix_kernel, d_rnn, d_att, n_blocks, seq_len // TOKEN_TILE),
        out_shape=(sds(d_att, BF16), tiles_sds, tiles_sds, sds(d_rnn, BF16)),
        grid=(t // TOKEN_TILE,),
        in_specs=[row(d), _const_spec((1, d)), _layer_spec(layer, (d, 2 * d_rnn + 3 * d_att)),
                  _const_spec((HEAD_DIM, TOKEN_TILE)), _const_spec((1, LANES)),
                  _const_spec(perm.shape), _const_spec(unperm.shape),
                  _const_spec((CONV_WIDTH, SUBLANES, c)), _const_spec((SUBLANES, c)),
                  _layer_spec(layer, (c, 2 * c)), _const_spec((1, c)), _const_spec((1, c)),
                  _const_spec((1, c))],
        out_specs=(row(d_att), tiles, tiles, row(d_rnn)),
        scratch_shapes=[pltpu.VMEM((TOKEN_TILE, 2 * c), F32),
                        pltpu.VMEM((CONV_WIDTH - 1, SUBLANES, c), F32),
                        pltpu.VMEM((1, c), F32), tile, tile, tile, tile],
        compiler_params=pltpu.CompilerParams(
            dimension_semantics=("arbitrary",), vmem_limit_bytes=56 * MIB),
        name="mix",
    )(x2d, g, w_in, q_gain_t, k_gain2, perm, unperm, conv_w, conv_b, w_ax, b_a, b_x, lam)


def _bias_kernel(base_ref, o_ref):
    for h in range(o_ref.shape[0]):
        for n in range(ATT_NBLK):
            if ATT_QB * n - (ATT_QB - 1) >= REL_CLIP:
                o_ref[h, n] = jnp.broadcast_to(base_ref[h, n][:, :ATT_QB], (ATT_QB, ATT_QB))
                continue
            x = jnp.broadcast_to(base_ref[h, n], (ATT_QB, 2 * ATT_QB))
            o_ref[h, n] = pltpu.roll(x, 0, 1, stride=1, stride_axis=0)[:, :ATT_QB]


def _bias_call(rel_table):
    nh = rel_table.shape[0]
    j = np.arange(2 * ATT_QB)
    rel = np.where(j < ATT_QB, j, j - 2 * ATT_QB)[None, :] + ATT_QB * np.arange(ATT_NBLK)[:, None]
    idx = np.clip(rel, -REL_CLIP, REL_CLIP) + REL_CLIP
    base = (rel_table.astype(F32) * LOG2E)[:, idx].reshape(nh, ATT_NBLK, 1, 2 * ATT_QB)
    return pl.pallas_call(
        _bias_kernel,
        out_shape=jax.ShapeDtypeStruct((nh, ATT_NBLK, ATT_QB, ATT_QB), F32),
        grid=(nh // ATT_HG,),
        in_specs=[pl.BlockSpec((ATT_HG, ATT_NBLK, 1, 2 * ATT_QB), lambda h: (h, 0, 0, 0))],
        out_specs=pl.BlockSpec((ATT_HG, ATT_NBLK, ATT_QB, ATT_QB), lambda h: (h, 0, 0, 0)),
        compiler_params=pltpu.CompilerParams(dimension_semantics=("parallel",)),
        name="bias",
    )(base)


def _slab_state(n, kc, half):
    per = ATT_QB // CHUNK
    ok = [0 <= per * n + qc - kc <= LEFT_CHUNKS
          for qc in range(half * (LANES // CHUNK), (half + 1) * (LANES // CHUNK))]
    return {(True, True): "all", (False, True): "hi", (True, False): "lo",
            (False, False): "none"}[tuple(ok)]


def _head_query(qt, h):
    lanes = ATT_HG * HEAD_DIM
    r0, r1 = h * HEAD_DIM, (h + 1) * HEAD_DIM
    pieces = []
    if r0:
        pieces.append(jnp.zeros((r0, ATT_QB), BF16))
    pieces.append(qt[r0:r1, :])
    if r1 < lanes:
        pieces.append(jnp.zeros((lanes - r1, ATT_QB), BF16))
    return jnp.concatenate(pieces, axis=0)


def _value_rows(vt_ref, tile, h):
    vt = vt_ref[0, tile, h * HEAD_DIM:(h + 1) * HEAD_DIM, :]
    return jnp.concatenate([vt, jnp.ones((2 * SUBLANES, ATT_QB), BF16)], axis=0)


def _attn_kernel(qt_ref, k_ref, vt_ref, bias_ref, mask_ref, o_ref):
    step = pl.program_id(2)
    halves = ATT_QB // LANES
    kchunks = ATT_QB // CHUNK

    def load_keys(tiles):
        starts = [t * ATT_QB if isinstance(t, int) else pl.multiple_of(t * ATT_QB, ATT_QB)
                  for t in tiles]
        return [k_ref[0, pl.ds(start, ATT_QB), :] for start in starts]

    def head_scores(j, tiles, h):
        qm = _head_query(qt_ref[0, j], h)
        return [_dot(kb, qm) for kb in load_keys(tiles)]

    def column_max(x):
        m = []
        for a in range(halves):
            top = functools.reduce(jnp.maximum, [v for (_, _, aa), v in x.items() if aa == a])
            m.append(jnp.max(top, axis=0, keepdims=True))
        return m

    def interior_scores(j, tiles, h):
        s = head_scores(j, tiles, h)
        hi = lax.broadcasted_iota(jnp.int32, (CHUNK, LANES), 1) >= CHUNK
        x = {}
        for n in range(ATT_NBLK):
            for kc in range(kchunks):
                for a in range(halves):
                    state = _slab_state(n, kc, a)
                    if state == "none":
                        continue
                    rs = slice(kc * CHUNK, (kc + 1) * CHUNK)
                    cs = slice(a * LANES, (a + 1) * LANES)
                    v = s[n][rs, cs] + bias_ref[h, n, rs, cs]
                    if state == "hi":
                        v = jnp.where(hi, v, -1e30)
                    elif state == "lo":
                        v = jnp.where(hi, -1e30, v)
                    x[n, kc, a] = v
        return x, column_max(x)

    def edge_scores(j, gi, tiles, h):
        s = head_scores(j, tiles, h)
        x = {}
        for n in range(ATT_NBLK):
            valid = mask_ref[n] > (0.5 if gi >= n else 2.0)
            xn = jnp.where(valid, s[n] + bias_ref[h, n], -1e30)
            for kc in range(kchunks):
                for a in range(halves):
                    x[n, kc, a] = xn[kc * CHUNK:(kc + 1) * CHUNK, a * LANES:(a + 1) * LANES]
        return x, column_max(x)

    def softmax_pv(tiles, h, x, m):
        o_ext = None
        for n in range(ATT_NBLK):
            rows = []
            for kc in range(kchunks):
                rows.append(jnp.concatenate(
                    [jnp.exp2((x[n, kc, a] - m[a]).astype(BF16)) if (n, kc, a) in x
                     else jnp.zeros((CHUNK, LANES), BF16) for a in range(halves)], axis=1))
            part = _dot(_value_rows(vt_ref, tiles[n], h), jnp.concatenate(rows, axis=0))
            o_ext = part if o_ext is None else o_ext + part
        return o_ext[:HEAD_DIM] * (1.0 / o_ext[HEAD_DIM:HEAD_DIM + 1])

    def run_units(first):
        base = 0 if first else step * ATT_SUB
        tiles = [[(max(j - n, 0) if first else base + j - n) for n in range(ATT_NBLK)]
                 for j in range(ATT_SUB)]

        def scores(j, h):
            if first and j < ATT_NBLK - 1:
                return edge_scores(j, j, tiles[j], h)
            return interior_scores(j, tiles[j], h)

        pending = [scores(0, h) for h in range(ATT_HG)]
        for j in range(ATT_SUB):
            outs = []
            for h in range(ATT_HG):
                x, m = pending[h]
                if j + 1 < ATT_SUB:
                    pending[h] = scores(j + 1, h)
                outs.append(softmax_pv(tiles[j], h, x, m))
            ot = jnp.concatenate(outs, axis=0)
            o_ref[0, j * ATT_QB:(j + 1) * ATT_QB, :] = ot.T.astype(BF16)

    @pl.when(step == 0)
    def _():
        run_units(True)

    @pl.when(step > 0)
    def _():
        run_units(False)


def _attn_call(qt, k, vt, bias, mask, layer):
    bsz, s, d_att = k.shape
    lanes = ATT_HG * HEAD_DIM
    rows = ATT_SUB * ATT_QB
    nt = s // ATT_QB
    return pl.pallas_call(
        _attn_kernel,
        out_shape=jax.ShapeDtypeStruct((bsz, s, d_att), BF16),
        grid=(bsz, d_att // lanes, s // rows),
        in_specs=[pl.BlockSpec((1, ATT_SUB, lanes, ATT_QB), lambda b, g, i: (b, i, g, 0)),
                  pl.BlockSpec((1, s, lanes), lambda b, g, i: (b, 0, g)),
                  pl.BlockSpec((1, nt, lanes, ATT_QB), lambda b, g, i: (b, 0, g, 0)),
                  pl.BlockSpec((ATT_HG, ATT_NBLK, ATT_QB, ATT_QB),
                               lambda b, g, i: (layer * (d_att // lanes) + g, 0, 0, 0)),
                  _const_spec((ATT_NBLK, ATT_QB, ATT_QB))],
        out_specs=pl.BlockSpec((1, rows, lanes), lambda b, g, i: (b, i, g)),
        compiler_params=pltpu.CompilerParams(
            dimension_semantics=("parallel", "parallel", "arbitrary"),
            vmem_limit_bytes=48 * MIB),
        name="attn",
    )(qt, k, vt, bias, mask)


def _attn_valid_mask():
    per = ATT_QB // CHUNK
    n = np.arange(ATT_NBLK)[:, None, None]
    kc = (np.arange(ATT_QB) // CHUNK)[None, :, None]
    qc = (np.arange(ATT_QB) // CHUNK)[None, None, :]
    dist = per * n + qc - kc
    return ((dist >= 0) & (dist <= LEFT_CHUNKS)).astype(np.float32)


def _merge_kernel(x_ref, za_ref, at_ref, gm_ref, wgate_ref, gbias_ref, wa_ref, wb_ref, wo_ref,
                  g_ref, wg_ref, wu_ref, wd_ref, o_ref):
    d = x_ref.shape[1]
    x = x_ref[...]
    gates = _dot(_rms_norm(x, gm_ref[...]).astype(BF16), wgate_ref[...]) + gbias_ref[...]
    y_a = _dot(za_ref[...], wa_ref[...])
    y_b = _dot(at_ref[...], wb_ref[...])
    mix = (jax.nn.sigmoid(gates[:, :d]) * y_a + jax.nn.sigmoid(gates[:, d:]) * y_b).astype(BF16)
    x = x + _dot(mix, wo_ref[...])
    o_ref[...] = _ffn_residual(x, g_ref, wg_ref, wu_ref, wd_ref)


def _merge_call(x2d, za, at, g_mix, w_gate, gate_bias, w_up_a, w_up_b, w_out, g, wg, wu, wd, layer):
    t, d = x2d.shape
    d_rnn, d_att, d_ff = za.shape[1], at.shape[1], wg.shape[-1]
    row = lambda w: pl.BlockSpec((TOKEN_TILE, w), lambda i: (i, 0))
    return pl.pallas_call(
        _merge_kernel,
        out_shape=jax.ShapeDtypeStruct((t, d), F32),
        grid=(t // TOKEN_TILE,),
        in_specs=[row(d), row(d_rnn), row(d_att), _const_spec((1, d)),
                  _layer_spec(layer, (d, 2 * d)), _const_spec((1, 2 * d)),
                  _layer_spec(layer, (d_rnn, d)), _layer_spec(layer, (d_att, d)),
                  _layer_spec(layer, (d, d)), _const_spec((1, d)), _layer_spec(layer, (d, d_ff)),
                  _layer_spec(layer, (d, d_ff)), _layer_spec(layer, (d_ff, d))],
        out_specs=row(d),
        compiler_params=pltpu.CompilerParams(
            dimension_semantics=("parallel",), vmem_limit_bytes=56 * MIB),
        name="merge",
    )(x2d, za, at, g_mix, w_gate, gate_bias, w_up_a, w_up_b, w_out, g, wg, wu, wd)


def _segment_permutation():
    seg = TIME_TILE // SUBLANES
    p = np.arange(TIME_TILE)
    perm = np.zeros((TIME_TILE, TIME_TILE), np.float32)
    perm[p, (p % SUBLANES) * seg + p // SUBLANES] = 1.0
    return perm


def _block_diag(w):
    depth, nb, n, _ = w.shape
    tiled = jnp.tile(w.reshape(depth, nb * n, n), (1, 1, nb))
    blk = np.arange(nb * n) // n
    return jnp.where(jnp.asarray(blk[:, None] == blk[None, :]), tiled, 0)


def kernel(x, norm_ffn1, ffn1_w_gate, ffn1_w_up, ffn1_w_down, norm_mix, w_in, gate_bias,
           conv_w, conv_b, rg_w_a, rg_b_a, rg_w_x, rg_b_x, rg_lambda, w_up_a,
           q_gain, k_gain, rel_table, w_up_b, w_out,
           norm_ffn2, ffn2_w_gate, ffn2_w_up, ffn2_w_down):
    bsz, s, d = x.shape
    depth = w_in.shape[0]
    d_rnn = conv_w.shape[-1]
    d_att = w_up_b.shape[1]
    assert q_gain.shape[-1] == HEAD_DIM and rel_table.shape[-1] == 2 * REL_CLIP + 1
    assert (bsz * s) % TOKEN_TILE == 0 and s % TIME_TILE == 0 and s % (ATT_SUB * ATT_QB) == 0
    assert s % TOKEN_TILE == 0 and d_att == d and TOKEN_TILE % TIME_TILE == 0

    mask = jnp.asarray(_attn_valid_mask())
    perm = _segment_permutation()
    perm, unperm = jnp.asarray(perm, BF16), jnp.asarray(perm.T, BF16)
    row = lambda a: a.reshape(1, -1).astype(F32)
    bf = lambda a: a.astype(BF16)
    rows8 = lambda a: jnp.broadcast_to(a.astype(F32), a.shape[:-2] + (SUBLANES, a.shape[-1]))
    q_scale = HEAD_DIM ** -0.5 * LOG2E

    w_in_b = bf(w_in)
    w_gate = w_in_b[:, :, 2 * d_rnn + 3 * d_att:]
    bias = _bias_call(rel_table.reshape(-1, rel_table.shape[-1]))
    w_ax = bf(jnp.concatenate([_block_diag(rg_w_a), _block_diag(rg_w_x)], axis=2))
    ffn1 = (bf(ffn1_w_gate), bf(ffn1_w_up), bf(ffn1_w_down))
    ffn2 = (bf(ffn2_w_gate), bf(ffn2_w_up), bf(ffn2_w_down))
    w_up_a_b, w_up_b_b, w_out_b = bf(w_up_a), bf(w_up_b), bf(w_out)

    x2d = x.reshape(bsz * s, d)
    for l in range(depth):
        x2d = _ffn_call(x2d, row(norm_ffn1[l]), *ffn1, l)
        q_gain_t = jnp.broadcast_to((q_gain[l].astype(F32) * q_scale)[:, None],
                                    (HEAD_DIM, TOKEN_TILE))
        k_gain2 = jnp.tile(row(k_gain[l]), (1, LANES // HEAD_DIM))
        k, qt, vt, za = _mix_call(
            x2d, row(norm_mix[l]), w_in_b, q_gain_t, k_gain2, perm, unperm,
            rows8(conv_w[l][:, None, :]), rows8(conv_b[l][None, :]), w_ax, row(rg_b_a[l]),
            row(rg_b_x[l]), row(rg_lambda[l]), s, d_rnn, d_att, rg_w_a.shape[1], l)
        at = _attn_call(qt.reshape(bsz, s // ATT_QB, d_att, ATT_QB), k.reshape(bsz, s, d_att),
                        vt.reshape(bsz, s // ATT_QB, d_att, ATT_QB),
                        bias, mask, l)
        x2d = _merge_call(x2d, za, at.reshape(bsz * s, d_att), row(norm_mix[l]), w_gate,
                          row(gate_bias[l]), w_up_a_b, w_up_b_b, w_out_b, row(norm_ffn2[l]),
                          *ffn2, l)
    return x2d.reshape(bsz, s, d)
```

```python
import functools

import numpy as np
import jax
import jax.numpy as jnp
from jax import lax
from jax.experimental import pallas as pl
from jax.experimental.pallas import tpu as pltpu

F32 = jnp.float32
BF16 = jnp.bfloat16

LANES = 128
SUBLANES = 8
MXU_DIM = 256
V7X_VMEM_BYTES = 64 * 1024 * 1024
VMEM_LIMIT_RESIDENT = V7X_VMEM_BYTES * 7 // 8
VMEM_LIMIT_STREAMING = V7X_VMEM_BYTES * 3 // 4

CHUNK = 64
LEFT_CHUNKS = 8
HEAD_DIM = 64
REL_CLIP = 128
CONV_WIDTH = 4
LRU_C = 8.0
EPS = 1e-6
LOG2E = 1.4426950408889634

TOKEN_TILE = 512
FFN_TOKEN_TILE = 1024
FF_TILE = 3 * MXU_DIM
TIME_TILE = 256
ATT_QB = MXU_DIM
ATT_NBLK = LEFT_CHUNKS * CHUNK // ATT_QB + 1
ATT_SUB = 8
ATT_HG = MXU_DIM // HEAD_DIM
assert (LEFT_CHUNKS * CHUNK) % ATT_QB == 0 and ATT_QB % CHUNK == 0 and TOKEN_TILE % ATT_QB == 0


def _const_spec(shape):
    nd = len(shape)
    return pl.BlockSpec(shape, lambda *_: (0,) * nd, pipeline_mode=pl.Buffered(1))


def _layer_spec(layer, shape):
    nd = len(shape)
    return pl.BlockSpec((None,) + tuple(shape), lambda *_: (layer,) + (0,) * nd,
                        pipeline_mode=pl.Buffered(1))


def _rms_norm(x, g):
    ms = jnp.mean(x * x, axis=-1, keepdims=True)
    return x * lax.rsqrt(ms + EPS) * g


def _dot(a, b):
    return jnp.dot(a, b, preferred_element_type=F32)


def _dot_tt(a, b):
    return lax.dot_general(a, b, (((0,), (1,)), ((), ())), preferred_element_type=F32)


def _ffn_residual(x, g_ref, wg_ref, wu_ref, wd_ref):
    d_ff = wg_ref.shape[1]
    h = _rms_norm(x, g_ref[...]).astype(BF16)
    acc = None
    for c0 in range(0, d_ff, FF_TILE):
        c1 = min(c0 + FF_TILE, d_ff)
        gate = _dot(h, wg_ref[:, c0:c1])
        up = _dot(h, wu_ref[:, c0:c1])
        act = (gate * jax.nn.sigmoid(gate) * up).astype(BF16)
        part = _dot(act, wd_ref[c0:c1, :])
        acc = part if acc is None else acc + part
    return x + 0.5 * acc


def _ffn_kernel(x_ref, g_ref, wg_ref, wu_ref, wd_ref, o_ref):
    o_ref[...] = _ffn_residual(x_ref[...], g_ref, wg_ref, wu_ref, wd_ref)


def _ffn_call(x2d, g, wg, wu, wd, layer):
    t, d = x2d.shape
    d_ff = wg.shape[-1]
    row = pl.BlockSpec((FFN_TOKEN_TILE, d), lambda i: (i, 0))
    return pl.pallas_call(
        _ffn_kernel,
        out_shape=jax.ShapeDtypeStruct((t, d), F32),
        grid=(t // FFN_TOKEN_TILE,),
        in_specs=[row, _const_spec((1, d)), _layer_spec(layer, (d, d_ff)),
                  _layer_spec(layer, (d, d_ff)), _layer_spec(layer, (d_ff, d))],
        out_specs=row,
        compiler_params=pltpu.CompilerParams(
            dimension_semantics=("parallel",), vmem_limit_bytes=VMEM_LIMIT_RESIDENT),
        name="ffn",
    )(x2d, g, wg, wu, wd)


def _head_rms_norm(x, gain2):
    rows, width = x.shape
    low = lax.broadcasted_iota(jnp.int32, (rows, LANES), 1) < HEAD_DIM
    out = []
    for c in range(0, width, LANES):
        xs = x[:, c:c + LANES]
        sq = xs * xs
        ms_lo = jnp.sum(jnp.where(low, sq, 0.0), axis=-1, keepdims=True) * (1.0 / HEAD_DIM)
        ms_hi = jnp.sum(jnp.where(low, 0.0, sq), axis=-1, keepdims=True) * (1.0 / HEAD_DIM)
        r = jnp.where(low, lax.rsqrt(ms_lo + EPS), lax.rsqrt(ms_hi + EPS))
        out.append(xs * r * gain2)
    return jnp.concatenate(out, axis=-1)


def _head_rms_norm_t(xt, gain_t):
    out = []
    for r0 in range(0, xt.shape[0], HEAD_DIM):
        xs = xt[r0:r0 + HEAD_DIM, :]
        ms = jnp.mean(xs * xs, axis=0, keepdims=True)
        out.append(xs * lax.rsqrt(ms + EPS) * gain_t)
    return jnp.concatenate(out, axis=0)


def _rnn_tile_steps(n_blocks, xy_ref, r0, d_rnn, cw_ref, cb_ref, wax_ref, ba_ref, bx_ref, lam_ref,
                    unperm_ref, za_ref, tail_ref, h_ref, u_ref, a_ref, b_ref, z_ref):
    rows, c = TIME_TILE, d_rnn
    seg = rows // SUBLANES
    taps = CONV_WIDTH - 1
    pieces = 4
    sub = lax.broadcasted_iota(jnp.int32, (SUBLANES, c), 0)

    def group(col0, j):
        return xy_ref[r0 + j * SUBLANES:r0 + (j + 1) * SUBLANES, col0:col0 + c]

    last = [group(0, seg - k) for k in range(1, taps + 1)]
    hist = [jnp.where(sub == 0, pltpu.roll(tail_ref[k], 1, axis=0), pltpu.roll(last[k], 1, axis=0))
            for k in range(taps - 1, -1, -1)]
    for k in range(taps):
        tail_ref[k] = last[k]
    for j in range(seg):
        hist.append(last[seg - 1 - j] if seg - 1 - j < taps else group(0, j))
        u = cb_ref[...]
        for t in range(CONV_WIDTH):
            u = u + hist[t] * cw_ref[t]
        u_ref[j * SUBLANES:(j + 1) * SUBLANES, :] = u
        hist.pop(0)
        if (j + 1) % (seg // pieces) == 0:
            yield

    n = c // n_blocks
    decay_rate = (-LRU_C) * jax.nn.softplus(-lam_ref[...])
    for c0 in range(0, c, MXU_DIM):
        c1 = min(c0 + MXU_DIM, c)
        w0 = (c0 // n) * n // LANES * LANES
        w1 = min(-(-(-(-c1 // n) * n) // LANES) * LANES, c)
        u_bf = u_ref[:, w0:w1].astype(BF16)
        r = jax.nn.sigmoid(_dot(u_bf, wax_ref[w0:w1, c0:c1]) + ba_ref[:, c0:c1])
        i = jax.nn.sigmoid(_dot(u_bf, wax_ref[w0:w1, c + c0:c + c1]) + bx_ref[:, c0:c1])
        log_a = r * decay_rate[:, c0:c1]
        a = jnp.exp(log_a)
        a_ref[:, c0:c1] = a
        b_ref[:, c0:c1] = jnp.sqrt(-jnp.tanh(log_a) * (a * a + 1.0)) * (i * u_ref[:, c0:c1])
        yield

    state = jnp.zeros((SUBLANES, c), F32)
    decay = jnp.ones((SUBLANES, c), F32)
    for j in range(seg):
        sl = slice(j * SUBLANES, (j + 1) * SUBLANES)
        a_j = a_ref[sl, :]
        state = a_j * state + b_ref[sl, :]
        decay = a_j * decay
        b_ref[sl, :] = state
        a_ref[sl, :] = decay
    yield

    d = 1
    while d < SUBLANES:
        keep = sub >= d
        decay_prev = jnp.where(keep, pltpu.roll(decay, d, axis=0), 1.0)
        state_prev = jnp.where(keep, pltpu.roll(state, d, axis=0), 0.0)
        state = decay * state_prev + state
        decay = decay * decay_prev
        d *= 2
    ends = decay * h_ref[...] + state
    carry = jnp.where(sub == 0, h_ref[...], pltpu.roll(ends, 1, axis=0))
    h_ref[...] = ends[SUBLANES - 1:SUBLANES, :]

    for j in range(seg):
        sl = slice(j * SUBLANES, (j + 1) * SUBLANES)
        z_ref[sl, :] = jax.nn.gelu(group(c, j)) * (b_ref[sl, :] + a_ref[sl, :] * carry)
        if (j + 1) % (seg // pieces) == 0:
            yield

    za_ref[r0:r0 + rows, :] = _dot(unperm_ref[...], z_ref[...].astype(BF16)).astype(BF16)
    yield


def _interleave(primary, secondary, ratio):
    live = True
    while live:
        live = next(primary, "done") != "done"
        for _ in range(ratio):
            live = (next(secondary, "done") != "done") or live


def _mix_kernel(d_rnn, d_att, n_blocks, steps_per_seq,
                x_ref, g_ref, w_ref, qg_ref, kg_ref, perm_ref, unperm_ref,
                cw_ref, cb_ref, wax_ref, ba_ref, bx_ref, lam_ref,
                k_ref, qt_ref, vt_ref, za_ref,
                xy_ref, tail_ref, h_ref, u_ref, a_ref, b_ref, z_ref):
    taps = CONV_WIDTH - 1
    rows = x_ref.shape[0]
    c_q = 2 * d_rnn

    @pl.when(pl.program_id(0) % steps_per_seq == 0)
    def _():
        tail_ref[...] = jnp.zeros((taps, SUBLANES, d_rnn), F32)
        h_ref[...] = jnp.zeros((1, d_rnn), F32)

    h = _rms_norm(x_ref[...], g_ref[...]).astype(BF16)

    def recurrent_inputs(r0):
        hp = _dot(perm_ref[...], h[r0:r0 + TIME_TILE, :]).astype(BF16)
        for c0 in range(0, c_q, FF_TILE):
            c1 = min(c0 + FF_TILE, c_q)
            xy_ref[r0:r0 + TIME_TILE, c0:c1] = _dot(hp, w_ref[:, c0:c1])
            yield

    def later_inputs():
        for r0 in range(TIME_TILE, rows, TIME_TILE):
            yield from recurrent_inputs(r0)

    def keys():
        for c0 in range(0, d_att, MXU_DIM):
            col = c_q + d_att + c0
            k = _dot(h, w_ref[:, col:col + MXU_DIM])
            k_ref[:, c0:c0 + MXU_DIM] = _head_rms_norm(k, kg_ref[...]).astype(BF16)
            yield

    def transposed(src0, out_ref, norm):
        for c0 in range(0, d_att, MXU_DIM):
            t = _dot_tt(w_ref[:, src0 + c0:src0 + c0 + MXU_DIM], h)
            if norm:
                t = _head_rms_norm_t(t, qg_ref[...])
            t = t.astype(BF16)
            for n in range(out_ref.shape[0]):
                out_ref[n, c0:c0 + MXU_DIM, :] = t[:, n * ATT_QB:(n + 1) * ATT_QB]
            yield

    def projections():
        family = {"x": later_inputs(), "k": keys(), "q": transposed(c_q, qt_ref, True),
                  "v": transposed(c_q + 2 * d_att, vt_ref, False)}
        for name in "kkxxxxkkqqvvvvqq":
            next(family[name])
            yield
        assert all(next(g, "end") == "end" for g in family.values())
        assert rows == 2 * TIME_TILE

    def recurrence():
        for r0 in range(0, rows, TIME_TILE):
            yield from _rnn_tile_steps(
                n_blocks, xy_ref, r0, d_rnn, cw_ref, cb_ref, wax_ref, ba_ref, bx_ref, lam_ref,
                unperm_ref, za_ref, tail_ref, h_ref, u_ref, a_ref, b_ref, z_ref)

    for _ in recurrent_inputs(0):
        pass
    _interleave(projections(), recurrence(), 2)


def _mix_call(x2d, g, w_in, q_gain_t, k_gain2, perm, unperm, conv_w, conv_b, w_ax, b_a, b_x, lam,
              seq_len, d_rnn, d_att, n_blocks, layer):
    t, d = x2d.shape
    nt = TOKEN_TILE // ATT_QB
    c = d_rnn
    row = lambda w: pl.BlockSpec((TOKEN_TILE, w), lambda i: (i, 0))
    sds = lambda w, dt: jax.ShapeDtypeStruct((t, w), dt)
    tiles = pl.BlockSpec((nt, d_att, ATT_QB), lambda i: (i, 0, 0))
    tiles_sds = jax.ShapeDtypeStruct((t // ATT_QB, d_att, ATT_QB), BF16)
    tile = pltpu.VMEM((TIME_TILE, c), F32)
    return pl.pallas_call(
        functools.partial(_mix_kernel, d_rnn, d_att, n_blocks, seq_len // TOKEN_TILE),
        out_shape=(sds(d_att, BF16), tiles_sds, tiles_sds, sds(d_rnn, BF16)),
        grid=(t // TOKEN_TILE,),
        in_specs=[row(d), _const_spec((1, d)), _layer_spec(layer, (d, 2 * d_rnn + 3 * d_att)),
                  _const_spec((HEAD_DIM, TOKEN_TILE)), _const_spec((1, LANES)),
                  _const_spec(perm.shape), _const_spec(unperm.shape),
                  _const_spec((CONV_WIDTH, SUBLANES, c)), _const_spec((SUBLANES, c)),
                  _layer_spec(layer, (c, 2 * c)), _const_spec((1, c)), _const_spec((1, c)),
                  _const_spec((1, c))],
        out_specs=(row(d_att), tiles, tiles, row(d_rnn)),
        scratch_shapes=[pltpu.VMEM((TOKEN_TILE, 2 * c), F32),
                        pltpu.VMEM((CONV_WIDTH - 1, SUBLANES, c), F32),
                        pltpu.VMEM((1, c), F32), tile, tile, tile, tile],
        compiler_params=pltpu.CompilerParams(
            dimension_semantics=("arbitrary",), vmem_limit_bytes=VMEM_LIMIT_RESIDENT),
        name="mix",
    )(x2d, g, w_in, q_gain_t, k_gain2, perm, unperm, conv_w, conv_b, w_ax, b_a, b_x, lam)


def _bias_kernel(base_ref, o_ref):
    for h in range(o_ref.shape[0]):
        for n in range(ATT_NBLK):
            if ATT_QB * n - (ATT_QB - 1) >= REL_CLIP:
                o_ref[h, n] = jnp.broadcast_to(base_ref[h, n][:, :ATT_QB], (ATT_QB, ATT_QB))
                continue
            x = jnp.broadcast_to(base_ref[h, n], (ATT_QB, 2 * ATT_QB))
            o_ref[h, n] = pltpu.roll(x, 0, 1, stride=1, stride_axis=0)[:, :ATT_QB]


def _bias_call(rel_table):
    nh = rel_table.shape[0]
    j = np.arange(2 * ATT_QB)
    rel = np.where(j < ATT_QB, j, j - 2 * ATT_QB)[None, :] + ATT_QB * np.arange(ATT_NBLK)[:, None]
    idx = np.clip(rel, -REL_CLIP, REL_CLIP) + REL_CLIP
    base = (rel_table.astype(F32) * LOG2E)[:, idx].reshape(nh, ATT_NBLK, 1, 2 * ATT_QB)
    return pl.pallas_call(
        _bias_kernel,
        out_shape=jax.ShapeDtypeStruct((nh, ATT_NBLK, ATT_QB, ATT_QB), F32),
        grid=(nh // ATT_HG,),
        in_specs=[pl.BlockSpec((ATT_HG, ATT_NBLK, 1, 2 * ATT_QB), lambda h: (h, 0, 0, 0))],
        out_specs=pl.BlockSpec((ATT_HG, ATT_NBLK, ATT_QB, ATT_QB), lambda h: (h, 0, 0, 0)),
        compiler_params=pltpu.CompilerParams(dimension_semantics=("parallel",)),
        name="bias",
    )(base)


def _slab_state(n, kc, half):
    per = ATT_QB // CHUNK
    ok = [0 <= per * n + qc - kc <= LEFT_CHUNKS
          for qc in range(half * (LANES // CHUNK), (half + 1) * (LANES // CHUNK))]
    return {(True, True): "all", (False, True): "hi", (True, False): "lo",
            (False, False): "none"}[tuple(ok)]


def _head_query(qt, h):
    lanes = ATT_HG * HEAD_DIM
    r0, r1 = h * HEAD_DIM, (h + 1) * HEAD_DIM
    pieces = []
    if r0:
        pieces.append(jnp.zeros((r0, ATT_QB), BF16))
    pieces.append(qt[r0:r1, :])
    if r1 < lanes:
        pieces.append(jnp.zeros((lanes - r1, ATT_QB), BF16))
    return jnp.concatenate(pieces, axis=0)


def _value_rows(vt_ref, tile, h):
    vt = vt_ref[0, tile, h * HEAD_DIM:(h + 1) * HEAD_DIM, :]
    return jnp.concatenate([vt, jnp.ones((2 * SUBLANES, ATT_QB), BF16)], axis=0)


def _attn_kernel(qt_ref, k_ref, vt_ref, bias_ref, mask_ref, o_ref):
    step = pl.program_id(2)
    halves = ATT_QB // LANES
    kchunks = ATT_QB // CHUNK

    def load_keys(tiles):
        starts = [t * ATT_QB if isinstance(t, int) else pl.multiple_of(t * ATT_QB, ATT_QB)
                  for t in tiles]
        return [k_ref[0, pl.ds(start, ATT_QB), :] for start in starts]

    def head_scores(j, tiles, h):
        qm = _head_query(qt_ref[0, j], h)
        return [_dot(kb, qm) for kb in load_keys(tiles)]

    def column_max(x):
        m = []
        for a in range(halves):
            top = functools.reduce(jnp.maximum, [v for (_, _, aa), v in x.items() if aa == a])
            m.append(jnp.max(top, axis=0, keepdims=True))
        return m

    def interior_scores(j, tiles, h):
        s = head_scores(j, tiles, h)
        hi = lax.broadcasted_iota(jnp.int32, (CHUNK, LANES), 1) >= CHUNK
        x = {}
        for n in range(ATT_NBLK):
            for kc in range(kchunks):
                for a in range(halves):
                    state = _slab_state(n, kc, a)
                    if state == "none":
                        continue
                    rs = slice(kc * CHUNK, (kc + 1) * CHUNK)
                    cs = slice(a * LANES, (a + 1) * LANES)
                    v = s[n][rs, cs] + bias_ref[h, n, rs, cs]
                    if state == "hi":
                        v = jnp.where(hi, v, -1e30)
                    elif state == "lo":
                        v = jnp.where(hi, -1e30, v)
                    x[n, kc, a] = v
        return x, column_max(x)

    def edge_scores(j, gi, tiles, h):
        s = head_scores(j, tiles, h)
        x = {}
        for n in range(ATT_NBLK):
            valid = mask_ref[n] > (0.5 if gi >= n else 2.0)
            xn = jnp.where(valid, s[n] + bias_ref[h, n], -1e30)
            for kc in range(kchunks):
                for a in range(halves):
                    x[n, kc, a] = xn[kc * CHUNK:(kc + 1) * CHUNK, a * LANES:(a + 1) * LANES]
        return x, column_max(x)

    def softmax_pv(tiles, h, x, m):
        o_ext = None
        for n in range(ATT_NBLK):
            rows = []
            for kc in range(kchunks):
                rows.append(jnp.concatenate(
                    [jnp.exp2((x[n, kc, a] - m[a]).astype(BF16)) if (n, kc, a) in x
                     else jnp.zeros((CHUNK, LANES), BF16) for a in range(halves)], axis=1))
            part = _dot(_value_rows(vt_ref, tiles[n], h), jnp.concatenate(rows, axis=0))
            o_ext = part if o_ext is None else o_ext + part
        return o_ext[:HEAD_DIM] * (1.0 / o_ext[HEAD_DIM:HEAD_DIM + 1])

    def run_units(first):
        base = 0 if first else step * ATT_SUB
        tiles = [[(max(j - n, 0) if first else base + j - n) for n in range(ATT_NBLK)]
                 for j in range(ATT_SUB)]

        def scores(j, h):
            if first and j < ATT_NBLK - 1:
                return edge_scores(j, j, tiles[j], h)
            return interior_scores(j, tiles[j], h)

        pending = [scores(0, h) for h in range(ATT_HG)]
        for j in range(ATT_SUB):
            outs = []
            for h in range(ATT_HG):
                x, m = pending[h]
                if j + 1 < ATT_SUB:
                    pending[h] = scores(j + 1, h)
                outs.append(softmax_pv(tiles[j], h, x, m))
            ot = jnp.concatenate(outs, axis=0)
            o_ref[0, j * ATT_QB:(j + 1) * ATT_QB, :] = ot.T.astype(BF16)

    @pl.when(step == 0)
    def _():
        run_units(True)

    @pl.when(step > 0)
    def _():
        run_units(False)


def _attn_call(qt, k, vt, bias, mask, layer):
    bsz, s, d_att = k.shape
    lanes = ATT_HG * HEAD_DIM
    rows = ATT_SUB * ATT_QB
    nt = s // ATT_QB
    return pl.pallas_call(
        _attn_kernel,
        out_shape=jax.ShapeDtypeStruct((bsz, s, d_att), BF16),
        grid=(d_att // lanes, bsz, s // rows),
        in_specs=[pl.BlockSpec((1, ATT_SUB, lanes, ATT_QB), lambda g, b, i: (b, i, g, 0)),
                  pl.BlockSpec((1, s, lanes), lambda g, b, i: (b, 0, g)),
                  pl.BlockSpec((1, nt, lanes, ATT_QB), lambda g, b, i: (b, 0, g, 0)),
                  pl.BlockSpec((ATT_HG, ATT_NBLK, ATT_QB, ATT_QB),
                               lambda g, b, i: (layer * (d_att // lanes) + g, 0, 0, 0)),
                  _const_spec((ATT_NBLK, ATT_QB, ATT_QB))],
        out_specs=pl.BlockSpec((1, rows, lanes), lambda g, b, i: (b, i, g)),
        compiler_params=pltpu.CompilerParams(
            dimension_semantics=("parallel", "parallel", "arbitrary"),
            vmem_limit_bytes=VMEM_LIMIT_STREAMING),
        name="attn",
    )(qt, k, vt, bias, mask)


def _attn_valid_mask():
    per = ATT_QB // CHUNK
    n = np.arange(ATT_NBLK)[:, None, None]
    kc = (np.arange(ATT_QB) // CHUNK)[None, :, None]
    qc = (np.arange(ATT_QB) // CHUNK)[None, None, :]
    dist = per * n + qc - kc
    return ((dist >= 0) & (dist <= LEFT_CHUNKS)).astype(np.float32)


def _merge_kernel(x_ref, za_ref, at_ref, gm_ref, wgate_ref, gbias_ref, wa_ref, wb_ref, wo_ref,
                  g_ref, wg_ref, wu_ref, wd_ref, o_ref):
    d = x_ref.shape[1]
    x = x_ref[...]
    gates = _dot(_rms_norm(x, gm_ref[...]).astype(BF16), wgate_ref[...]) + gbias_ref[...]
    y_a = _dot(za_ref[...], wa_ref[...])
    y_b = _dot(at_ref[...], wb_ref[...])
    mix = (jax.nn.sigmoid(gates[:, :d]) * y_a + jax.nn.sigmoid(gates[:, d:]) * y_b).astype(BF16)
    x = x + _dot(mix, wo_ref[...])
    o_ref[...] = _ffn_residual(x, g_ref, wg_ref, wu_ref, wd_ref)


def _merge_call(x2d, za, at, g_mix, w_gate, gate_bias, w_up_a, w_up_b, w_out, g, wg, wu, wd, layer):
    t, d = x2d.shape
    d_rnn, d_att, d_ff = za.shape[1], at.shape[1], wg.shape[-1]
    row = lambda w: pl.BlockSpec((TOKEN_TILE, w), lambda i: (i, 0))
    return pl.pallas_call(
        _merge_kernel,
        out_shape=jax.ShapeDtypeStruct((t, d), F32),
        grid=(t // TOKEN_TILE,),
        in_specs=[row(d), row(d_rnn), row(d_att), _const_spec((1, d)),
                  _layer_spec(layer, (d, 2 * d)), _const_spec((1, 2 * d)),
                  _layer_spec(layer, (d_rnn, d)), _layer_spec(layer, (d_att, d)),
                  _layer_spec(layer, (d, d)), _const_spec((1, d)), _layer_spec(layer, (d, d_ff)),
                  _layer_spec(layer, (d, d_ff)), _layer_spec(layer, (d_ff, d))],
        out_specs=row(d),
        compiler_params=pltpu.CompilerParams(
            dimension_semantics=("parallel",), vmem_limit_bytes=VMEM_LIMIT_RESIDENT),
        name="merge",
    )(x2d, za, at, g_mix, w_gate, gate_bias, w_up_a, w_up_b, w_out, g, wg, wu, wd)


def _segment_permutation():
    seg = TIME_TILE // SUBLANES
    p = np.arange(TIME_TILE)
    perm = np.zeros((TIME_TILE, TIME_TILE), np.float32)
    perm[p, (p % SUBLANES) * seg + p // SUBLANES] = 1.0
    return perm


def _block_diag(w):
    depth, nb, n, _ = w.shape
    tiled = jnp.tile(w.reshape(depth, nb * n, n), (1, 1, nb))
    blk = np.arange(nb * n) // n
    return jnp.where(jnp.asarray(blk[:, None] == blk[None, :]), tiled, 0)


def kernel(x, norm_ffn1, ffn1_w_gate, ffn1_w_up, ffn1_w_down, norm_mix, w_in, gate_bias,
           conv_w, conv_b, rg_w_a, rg_b_a, rg_w_x, rg_b_x, rg_lambda, w_up_a,
           q_gain, k_gain, rel_table, w_up_b, w_out,
           norm_ffn2, ffn2_w_gate, ffn2_w_up, ffn2_w_down):
    bsz, s, d = x.shape
    depth = w_in.shape[0]
    d_rnn = conv_w.shape[-1]
    d_att = w_up_b.shape[1]
    assert q_gain.shape[-1] == HEAD_DIM and rel_table.shape[-1] == 2 * REL_CLIP + 1
    assert (bsz * s) % FFN_TOKEN_TILE == 0 and s % TIME_TILE == 0 and s % (ATT_SUB * ATT_QB) == 0
    assert s % TOKEN_TILE == 0 and d_att == d and TOKEN_TILE % TIME_TILE == 0

    mask = jnp.asarray(_attn_valid_mask())
    perm = _segment_permutation()
    perm, unperm = jnp.asarray(perm, BF16), jnp.asarray(perm.T, BF16)
    row = lambda a: a.reshape(1, -1).astype(F32)
    bf = lambda a: a.astype(BF16)
    rows8 = lambda a: jnp.broadcast_to(a.astype(F32), a.shape[:-2] + (SUBLANES, a.shape[-1]))
    q_scale = HEAD_DIM ** -0.5 * LOG2E

    w_in_b = bf(w_in)
    w_gate = w_in_b[:, :, 2 * d_rnn + 3 * d_att:]
    bias = _bias_call(rel_table.reshape(-1, rel_table.shape[-1]))
    w_ax = bf(jnp.concatenate([_block_diag(rg_w_a), _block_diag(rg_w_x)], axis=2))
    ffn1 = (bf(ffn1_w_gate), bf(ffn1_w_up), bf(ffn1_w_down))
    ffn2 = (bf(ffn2_w_gate), bf(ffn2_w_up), bf(ffn2_w_down))
    w_up_a_b, w_up_b_b, w_out_b = bf(w_up_a), bf(w_up_b), bf(w_out)

    x2d = x.reshape(bsz * s, d)
    for l in range(depth):
        x2d = _ffn_call(x2d, row(norm_ffn1[l]), *ffn1, l)
        q_gain_t = jnp.broadcast_to((q_gain[l].astype(F32) * q_scale)[:, None],
                                    (HEAD_DIM, TOKEN_TILE))
        k_gain2 = jnp.tile(row(k_gain[l]), (1, LANES // HEAD_DIM))
        k, qt, vt, za = _mix_call(
            x2d, row(norm_mix[l]), w_in_b, q_gain_t, k_gain2, perm, unperm,
            rows8(conv_w[l][:, None, :]), rows8(conv_b[l][None, :]), w_ax, row(rg_b_a[l]),
            row(rg_b_x[l]), row(rg_lambda[l]), s, d_rnn, d_att, rg_w_a.shape[1], l)
        at = _attn_call(qt.reshape(bsz, s // ATT_QB, d_att, ATT_QB), k.reshape(bsz, s, d_att),
                        vt.reshape(bsz, s // ATT_QB, d_att, ATT_QB),
                        bias, mask, l)
        x2d = _merge_call(x2d, za, at.reshape(bsz * s, d_att), row(norm_mix[l]), w_gate,
                          row(gate_bias[l]), w_up_a_b, w_up_b_b, w_out_b, row(norm_ffn2[l]),
                          *ffn2, l)
    return x2d.reshape(bsz, s, d)
```

```python
import functools

import numpy as np
import jax
import jax.numpy as jnp
from jax import lax
from jax.experimental import pallas as pl
from jax.experimental.pallas import tpu as pltpu

F32 = jnp.float32
BF16 = jnp.bfloat16

LANES = 128
SUBLANES = 8
MXU_DIM = 256
V7X_VMEM_BYTES = 64 * 1024 * 1024
VMEM_LIMIT_RESIDENT = V7X_VMEM_BYTES * 7 // 8
VMEM_LIMIT_STREAMING = V7X_VMEM_BYTES * 3 // 4

CHUNK = 64
LEFT_CHUNKS = 8
HEAD_DIM = 64
REL_CLIP = 128
CONV_WIDTH = 4
LRU_C = 8.0
EPS = 1e-6
LOG2E = 1.4426950408889634

TOKEN_TILE = 512
FFN_TOKEN_TILE = 1024
FF_TILE = 3 * MXU_DIM
TIME_TILE = 256
ATT_QB = MXU_DIM
ATT_NBLK = LEFT_CHUNKS * CHUNK // ATT_QB + 1
ATT_SUB = 8
ATT_HG = MXU_DIM // HEAD_DIM
ATT_AHEAD = 3
assert (LEFT_CHUNKS * CHUNK) % ATT_QB == 0 and ATT_QB % CHUNK == 0 and TOKEN_TILE % ATT_QB == 0


def _const_spec(shape):
    nd = len(shape)
    return pl.BlockSpec(shape, lambda *_: (0,) * nd, pipeline_mode=pl.Buffered(1))


def _layer_spec(layer, shape):
    nd = len(shape)
    return pl.BlockSpec((None,) + tuple(shape), lambda *_: (layer,) + (0,) * nd,
                        pipeline_mode=pl.Buffered(1))


def _rms_norm(x, g):
    ms = jnp.mean(x * x, axis=-1, keepdims=True)
    return x * lax.rsqrt(ms + EPS) * g


def _dot(a, b):
    return jnp.dot(a, b, preferred_element_type=F32)


def _dot_tt(a, b):
    return lax.dot_general(a, b, (((0,), (1,)), ((), ())), preferred_element_type=F32)


def _ffn_residual(x, g_ref, wg_ref, wu_ref, wd_ref):
    d_ff = wg_ref.shape[1]
    h = _rms_norm(x, g_ref[...]).astype(BF16)
    acc = None
    for c0 in range(0, d_ff, FF_TILE):
        c1 = min(c0 + FF_TILE, d_ff)
        gate = _dot(h, wg_ref[:, c0:c1])
        up = _dot(h, wu_ref[:, c0:c1])
        act = (gate * jax.nn.sigmoid(gate) * up).astype(BF16)
        part = _dot(act, wd_ref[c0:c1, :])
        acc = part if acc is None else acc + part
    return x + 0.5 * acc


def _ffn_kernel(x_ref, g_ref, wg_ref, wu_ref, wd_ref, o_ref):
    o_ref[...] = _ffn_residual(x_ref[...], g_ref, wg_ref, wu_ref, wd_ref)


def _ffn_call(x2d, g, wg, wu, wd, layer):
    t, d = x2d.shape
    d_ff = wg.shape[-1]
    row = pl.BlockSpec((FFN_TOKEN_TILE, d), lambda i: (i, 0))
    return pl.pallas_call(
        _ffn_kernel,
        out_shape=jax.ShapeDtypeStruct((t, d), F32),
        grid=(t // FFN_TOKEN_TILE,),
        in_specs=[row, _const_spec((1, d)), _layer_spec(layer, (d, d_ff)),
                  _layer_spec(layer, (d, d_ff)), _layer_spec(layer, (d_ff, d))],
        out_specs=row,
        compiler_params=pltpu.CompilerParams(
            dimension_semantics=("parallel",), vmem_limit_bytes=VMEM_LIMIT_RESIDENT),
        name="ffn",
    )(x2d, g, wg, wu, wd)


def _head_rms_norm(x, gain2):
    rows, width = x.shape
    low = lax.broadcasted_iota(jnp.int32, (rows, LANES), 1) < HEAD_DIM
    out = []
    for c in range(0, width, LANES):
        xs = x[:, c:c + LANES]
        sq = xs * xs
        ms_lo = jnp.sum(jnp.where(low, sq, 0.0), axis=-1, keepdims=True) * (1.0 / HEAD_DIM)
        ms_hi = jnp.sum(jnp.where(low, 0.0, sq), axis=-1, keepdims=True) * (1.0 / HEAD_DIM)
        r = jnp.where(low, lax.rsqrt(ms_lo + EPS), lax.rsqrt(ms_hi + EPS))
        out.append(xs * r * gain2)
    return jnp.concatenate(out, axis=-1)


def _head_rms_norm_t(xt, gain_t):
    out = []
    for r0 in range(0, xt.shape[0], HEAD_DIM):
        xs = xt[r0:r0 + HEAD_DIM, :]
        ms = jnp.mean(xs * xs, axis=0, keepdims=True)
        out.append(xs * lax.rsqrt(ms + EPS) * gain_t)
    return jnp.concatenate(out, axis=0)


def _rnn_tile_steps(n_blocks, xy_ref, r0, d_rnn, cw_ref, cb_ref, wax_ref, ba_ref, bx_ref, lam_ref,
                    unperm_ref, za_ref, tail_ref, h_ref, u_ref, a_ref, b_ref, z_ref):
    rows, c = TIME_TILE, d_rnn
    seg = rows // SUBLANES
    taps = CONV_WIDTH - 1
    pieces = 4
    sub = lax.broadcasted_iota(jnp.int32, (SUBLANES, c), 0)

    def group(col0, j):
        return xy_ref[r0 + j * SUBLANES:r0 + (j + 1) * SUBLANES, col0:col0 + c]

    last = [group(0, seg - k) for k in range(1, taps + 1)]
    hist = [jnp.where(sub == 0, pltpu.roll(tail_ref[k], 1, axis=0), pltpu.roll(last[k], 1, axis=0))
            for k in range(taps - 1, -1, -1)]
    for k in range(taps):
        tail_ref[k] = last[k]
    for j in range(seg):
        hist.append(last[seg - 1 - j] if seg - 1 - j < taps else group(0, j))
        u = cb_ref[...]
        for t in range(CONV_WIDTH):
            u = u + hist[t] * cw_ref[t]
        u_ref[j * SUBLANES:(j + 1) * SUBLANES, :] = u
        hist.pop(0)
        if (j + 1) % (seg // pieces) == 0:
            yield

    n = c // n_blocks
    decay_rate = (-LRU_C) * jax.nn.softplus(-lam_ref[...])
    for c0 in range(0, c, MXU_DIM):
        c1 = min(c0 + MXU_DIM, c)
        w0 = (c0 // n) * n // LANES * LANES
        w1 = min(-(-(-(-c1 // n) * n) // LANES) * LANES, c)
        u_bf = u_ref[:, w0:w1].astype(BF16)
        r = jax.nn.sigmoid(_dot(u_bf, wax_ref[w0:w1, c0:c1]) + ba_ref[:, c0:c1])
        i = jax.nn.sigmoid(_dot(u_bf, wax_ref[w0:w1, c + c0:c + c1]) + bx_ref[:, c0:c1])
        log_a = r * decay_rate[:, c0:c1]
        a = jnp.exp(log_a)
        a_ref[:, c0:c1] = a
        b_ref[:, c0:c1] = jnp.sqrt(-jnp.tanh(log_a) * (a * a + 1.0)) * (i * u_ref[:, c0:c1])
        yield

    state = jnp.zeros((SUBLANES, c), F32)
    decay = jnp.ones((SUBLANES, c), F32)
    for j in range(seg):
        sl = slice(j * SUBLANES, (j + 1) * SUBLANES)
        a_j = a_ref[sl, :]
        state = a_j * state + b_ref[sl, :]
        decay = a_j * decay
        b_ref[sl, :] = state
        a_ref[sl, :] = decay
    yield

    d = 1
    while d < SUBLANES:
        keep = sub >= d
        decay_prev = jnp.where(keep, pltpu.roll(decay, d, axis=0), 1.0)
        state_prev = jnp.where(keep, pltpu.roll(state, d, axis=0), 0.0)
        state = decay * state_prev + state
        decay = decay * decay_prev
        d *= 2
    ends = decay * h_ref[...] + state
    carry = jnp.where(sub == 0, h_ref[...], pltpu.roll(ends, 1, axis=0))
    h_ref[...] = ends[SUBLANES - 1:SUBLANES, :]

    for j in range(seg):
        sl = slice(j * SUBLANES, (j + 1) * SUBLANES)
        z_ref[sl, :] = jax.nn.gelu(group(c, j)) * (b_ref[sl, :] + a_ref[sl, :] * carry)
        if (j + 1) % (seg // pieces) == 0:
            yield

    za_ref[r0:r0 + rows, :] = _dot(unperm_ref[...], z_ref[...].astype(BF16)).astype(BF16)
    yield


def _interleave(primary, secondary, ratio):
    live = True
    while live:
        live = next(primary, "done") != "done"
        for _ in range(ratio):
            live = (next(secondary, "done") != "done") or live


def _mix_kernel(d_rnn, d_att, n_blocks, steps_per_seq,
                x_ref, g_ref, w_ref, qg_ref, kg_ref, perm_ref, unperm_ref,
                cw_ref, cb_ref, wax_ref, ba_ref, bx_ref, lam_ref,
                k_ref, qt_ref, vt_ref, za_ref,
                xy_ref, tail_ref, h_ref, u_ref, a_ref, b_ref, z_ref):
    taps = CONV_WIDTH - 1
    rows = x_ref.shape[0]
    c_q = 2 * d_rnn

    @pl.when(pl.program_id(0) % steps_per_seq == 0)
    def _():
        tail_ref[...] = jnp.zeros((taps, SUBLANES, d_rnn), F32)
        h_ref[...] = jnp.zeros((1, d_rnn), F32)

    h = _rms_norm(x_ref[...], g_ref[...]).astype(BF16)

    def recurrent_inputs(r0):
        hp = _dot(perm_ref[...], h[r0:r0 + TIME_TILE, :]).astype(BF16)
        for c0 in range(0, c_q, FF_TILE):
            c1 = min(c0 + FF_TILE, c_q)
            xy_ref[r0:r0 + TIME_TILE, c0:c1] = _dot(hp, w_ref[:, c0:c1])
            yield

    def later_inputs():
        for r0 in range(TIME_TILE, rows, TIME_TILE):
            yield from recurrent_inputs(r0)

    def keys():
        for c0 in range(0, d_att, MXU_DIM):
            col = c_q + d_att + c0
            k = _dot(h, w_ref[:, col:col + MXU_DIM])
            k_ref[:, c0:c0 + MXU_DIM] = _head_rms_norm(k, kg_ref[...]).astype(BF16)
            yield

    def transposed(src0, out_ref, norm):
        for c0 in range(0, d_att, MXU_DIM):
            t = _dot_tt(w_ref[:, src0 + c0:src0 + c0 + MXU_DIM], h)
            if norm:
                t = _head_rms_norm_t(t, qg_ref[...])
            t = t.astype(BF16)
            for n in range(out_ref.shape[0]):
                out_ref[n, c0:c0 + MXU_DIM, :] = t[:, n * ATT_QB:(n + 1) * ATT_QB]
            yield

    def projections():
        family = {"x": later_inputs(), "k": keys(), "q": transposed(c_q, qt_ref, True),
                  "v": transposed(c_q + 2 * d_att, vt_ref, False)}
        for name in "kkxxxxkkqqvvvvqq":
            next(family[name])
            yield
        assert all(next(g, "end") == "end" for g in family.values())
        assert rows == 2 * TIME_TILE

    def recurrence():
        for r0 in range(0, rows, TIME_TILE):
            yield from _rnn_tile_steps(
                n_blocks, xy_ref, r0, d_rnn, cw_ref, cb_ref, wax_ref, ba_ref, bx_ref, lam_ref,
                unperm_ref, za_ref, tail_ref, h_ref, u_ref, a_ref, b_ref, z_ref)

    for _ in recurrent_inputs(0):
        pass
    _interleave(projections(), recurrence(), 2)


def _mix_call(x2d, g, w_in, q_gain_t, k_gain2, perm, unperm, conv_w, conv_b, w_ax, b_a, b_x, lam,
              seq_len, d_rnn, d_att, n_blocks, layer):
    t, d = x2d.shape
    nt = TOKEN_TILE // ATT_QB
    c = d_rnn
    row = lambda w: pl.BlockSpec((TOKEN_TILE, w), lambda i: (i, 0))
    sds = lambda w, dt: jax.ShapeDtypeStruct((t, w), dt)
    tiles = pl.BlockSpec((nt, d_att, ATT_QB), lambda i: (i, 0, 0))
    tiles_sds = jax.ShapeDtypeStruct((t // ATT_QB, d_att, ATT_QB), BF16)
    tile = pltpu.VMEM((TIME_TILE, c), F32)
    return pl.pallas_call(
        functools.partial(_mix_kernel, d_rnn, d_att, n_blocks, seq_len // TOKEN_TILE),
        out_shape=(sds(d_att, BF16), tiles_sds, tiles_sds, sds(d_rnn, BF16)),
        grid=(t // TOKEN_TILE,),
        in_specs=[row(d), _const_spec((1, d)), _layer_spec(layer, (d, 2 * d_rnn + 3 * d_att)),
                  _const_spec((HEAD_DIM, TOKEN_TILE)), _const_spec((1, LANES)),
                  _const_spec(perm.shape), _const_spec(unperm.shape),
                  _const_spec((CONV_WIDTH, SUBLANES, c)), _const_spec((SUBLANES, c)),
                  _layer_spec(layer, (c, 2 * c)), _const_spec((1, c)), _const_spec((1, c)),
                  _const_spec((1, c))],
        out_specs=(row(d_att), tiles, tiles, row(d_rnn)),
        scratch_shapes=[pltpu.VMEM((TOKEN_TILE, 2 * c), F32),
                        pltpu.VMEM((CONV_WIDTH - 1, SUBLANES, c), F32),
                        pltpu.VMEM((1, c), F32), tile, tile, tile, tile],
        compiler_params=pltpu.CompilerParams(
            dimension_semantics=("arbitrary",), vmem_limit_bytes=VMEM_LIMIT_RESIDENT),
        name="mix",
    )(x2d, g, w_in, q_gain_t, k_gain2, perm, unperm, conv_w, conv_b, w_ax, b_a, b_x, lam)


def _bias_kernel(base_ref, o_ref):
    for h in range(o_ref.shape[0]):
        for n in range(ATT_NBLK):
            if ATT_QB * n - (ATT_QB - 1) >= REL_CLIP:
                o_ref[h, n] = jnp.broadcast_to(base_ref[h, n][:, :ATT_QB], (ATT_QB, ATT_QB))
                continue
            x = jnp.broadcast_to(base_ref[h, n], (ATT_QB, 2 * ATT_QB))
            o_ref[h, n] = pltpu.roll(x, 0, 1, stride=1, stride_axis=0)[:, :ATT_QB]


def _bias_call(rel_table):
    nh = rel_table.shape[0]
    j = np.arange(2 * ATT_QB)
    rel = np.where(j < ATT_QB, j, j - 2 * ATT_QB)[None, :] + ATT_QB * np.arange(ATT_NBLK)[:, None]
    idx = np.clip(rel, -REL_CLIP, REL_CLIP) + REL_CLIP
    base = (rel_table.astype(F32) * LOG2E)[:, idx].reshape(nh, ATT_NBLK, 1, 2 * ATT_QB)
    return pl.pallas_call(
        _bias_kernel,
        out_shape=jax.ShapeDtypeStruct((nh, ATT_NBLK, ATT_QB, ATT_QB), F32),
        grid=(nh // ATT_HG,),
        in_specs=[pl.BlockSpec((ATT_HG, ATT_NBLK, 1, 2 * ATT_QB), lambda h: (h, 0, 0, 0))],
        out_specs=pl.BlockSpec((ATT_HG, ATT_NBLK, ATT_QB, ATT_QB), lambda h: (h, 0, 0, 0)),
        compiler_params=pltpu.CompilerParams(dimension_semantics=("parallel",)),
        name="bias",
    )(base)


def _slab_state(n, kc, half):
    per = ATT_QB // CHUNK
    ok = [0 <= per * n + qc - kc <= LEFT_CHUNKS
          for qc in range(half * (LANES // CHUNK), (half + 1) * (LANES // CHUNK))]
    return {(True, True): "all", (False, True): "hi", (True, False): "lo",
            (False, False): "none"}[tuple(ok)]


def _head_query(qt, h):
    lanes = ATT_HG * HEAD_DIM
    r0, r1 = h * HEAD_DIM, (h + 1) * HEAD_DIM
    pieces = []
    if r0:
        pieces.append(jnp.zeros((r0, ATT_QB), BF16))
    pieces.append(qt[r0:r1, :])
    if r1 < lanes:
        pieces.append(jnp.zeros((lanes - r1, ATT_QB), BF16))
    return jnp.concatenate(pieces, axis=0)


def _value_rows(vt_ref, tile, h):
    vt = vt_ref[0, tile, h * HEAD_DIM:(h + 1) * HEAD_DIM, :]
    return jnp.concatenate([vt, jnp.ones((2 * SUBLANES, ATT_QB), BF16)], axis=0)


def _attn_kernel(qt_ref, k_ref, vt_ref, bias_ref, mask_ref, o_ref):
    step = pl.program_id(2)
    halves = ATT_QB // LANES
    kchunks = ATT_QB // CHUNK

    def load_keys(tiles):
        starts = [t * ATT_QB if isinstance(t, int) else pl.multiple_of(t * ATT_QB, ATT_QB)
                  for t in tiles]
        return [k_ref[0, pl.ds(start, ATT_QB), :] for start in starts]

    def head_scores(j, tiles, h):
        qm = _head_query(qt_ref[0, j], h)
        return [_dot(kb, qm) for kb in load_keys(tiles)]

    def column_max(x):
        m = []
        for a in range(halves):
            top = functools.reduce(jnp.maximum, [v for (_, _, aa), v in x.items() if aa == a])
            m.append(jnp.max(top, axis=0, keepdims=True))
        return m

    def interior_scores(j, tiles, h):
        s = head_scores(j, tiles, h)
        hi = lax.broadcasted_iota(jnp.int32, (CHUNK, LANES), 1) >= CHUNK
        x = {}
        for n in range(ATT_NBLK):
            for kc in range(kchunks):
                for a in range(halves):
                    state = _slab_state(n, kc, a)
                    if state == "none":
                        continue
                    rs = slice(kc * CHUNK, (kc + 1) * CHUNK)
                    cs = slice(a * LANES, (a + 1) * LANES)
                    v = s[n][rs, cs] + bias_ref[h, n, rs, cs]
                    if state == "hi":
                        v = jnp.where(hi, v, -1e30)
                    elif state == "lo":
                        v = jnp.where(hi, -1e30, v)
                    x[n, kc, a] = v
        return x, column_max(x)

    def edge_scores(j, gi, tiles, h):
        s = head_scores(j, tiles, h)
        x = {}
        for n in range(ATT_NBLK):
            valid = mask_ref[n] > (0.5 if gi >= n else 2.0)
            xn = jnp.where(valid, s[n] + bias_ref[h, n], -1e30)
            for kc in range(kchunks):
                for a in range(halves):
                    x[n, kc, a] = xn[kc * CHUNK:(kc + 1) * CHUNK, a * LANES:(a + 1) * LANES]
        return x, column_max(x)

    def softmax_pv(tiles, h, x, m):
        o_ext = None
        for n in range(ATT_NBLK):
            rows = []
            for kc in range(kchunks):
                rows.append(jnp.concatenate(
                    [jnp.exp2((x[n, kc, a] - m[a]).astype(BF16)) if (n, kc, a) in x
                     else jnp.zeros((CHUNK, LANES), BF16) for a in range(halves)], axis=1))
            part = _dot(_value_rows(vt_ref, tiles[n], h), jnp.concatenate(rows, axis=0))
            o_ext = part if o_ext is None else o_ext + part
        return o_ext[:HEAD_DIM] * (1.0 / o_ext[HEAD_DIM:HEAD_DIM + 1])

    def run_units(first):
        base = 0 if first else step * ATT_SUB
        tiles = [[(max(j - n, 0) if first else base + j - n) for n in range(ATT_NBLK)]
                 for j in range(ATT_SUB)]

        def scores(j, h):
            if first and j < ATT_NBLK - 1:
                return edge_scores(j, j, tiles[j], h)
            return interior_scores(j, tiles[j], h)

        total = ATT_SUB * ATT_HG
        pending = [scores(t // ATT_HG, t % ATT_HG) for t in range(ATT_AHEAD)]
        outs = []
        for t in range(total):
            j, h = divmod(t, ATT_HG)
            x, m = pending.pop(0)
            if t + ATT_AHEAD < total:
                pending.append(scores((t + ATT_AHEAD) // ATT_HG, (t + ATT_AHEAD) % ATT_HG))
            outs.append(softmax_pv(tiles[j], h, x, m))
            if h == ATT_HG - 1:
                ot = jnp.concatenate(outs, axis=0)
                o_ref[0, j * ATT_QB:(j + 1) * ATT_QB, :] = ot.T.astype(BF16)
                outs = []

    @pl.when(step == 0)
    def _():
        run_units(True)

    @pl.when(step > 0)
    def _():
        run_units(False)


def _attn_call(qt, k, vt, bias, mask, layer):
    bsz, s, d_att = k.shape
    lanes = ATT_HG * HEAD_DIM
    rows = ATT_SUB * ATT_QB
    nt = s // ATT_QB
    return pl.pallas_call(
        _attn_kernel,
        out_shape=jax.ShapeDtypeStruct((bsz, s, d_att), BF16),
        grid=(d_att // lanes, bsz, s // rows),
        in_specs=[pl.BlockSpec((1, ATT_SUB, lanes, ATT_QB), lambda g, b, i: (b, i, g, 0)),
                  pl.BlockSpec((1, s, lanes), lambda g, b, i: (b, 0, g)),
                  pl.BlockSpec((1, nt, lanes, ATT_QB), lambda g, b, i: (b, 0, g, 0)),
                  pl.BlockSpec((ATT_HG, ATT_NBLK, ATT_QB, ATT_QB),
                               lambda g, b, i: (layer * (d_att // lanes) + g, 0, 0, 0)),
                  _const_spec((ATT_NBLK, ATT_QB, ATT_QB))],
        out_specs=pl.BlockSpec((1, rows, lanes), lambda g, b, i: (b, i, g)),
        compiler_params=pltpu.CompilerParams(
            dimension_semantics=("parallel", "parallel", "arbitrary"),
            vmem_limit_bytes=VMEM_LIMIT_STREAMING),
        name="attn",
    )(qt, k, vt, bias, mask)


def _attn_valid_mask():
    per = ATT_QB // CHUNK
    n = np.arange(ATT_NBLK)[:, None, None]
    kc = (np.arange(ATT_QB) // CHUNK)[None, :, None]
    qc = (np.arange(ATT_QB) // CHUNK)[None, None, :]
    dist = per * n + qc - kc
    return ((dist >= 0) & (dist <= LEFT_CHUNKS)).astype(np.float32)


def _merge_kernel(x_ref, za_ref, at_ref, gm_ref, wgate_ref, gbias_ref, wa_ref, wb_ref, wo_ref,
                  g_ref, wg_ref, wu_ref, wd_ref, o_ref):
    d = x_ref.shape[1]
    x = x_ref[...]
    gates = _dot(_rms_norm(x, gm_ref[...]).astype(BF16), wgate_ref[...]) + gbias_ref[...]
    y_a = _dot(za_ref[...], wa_ref[...])
    y_b = _dot(at_ref[...], wb_ref[...])
    mix = (jax.nn.sigmoid(gates[:, :d]) * y_a + jax.nn.sigmoid(gates[:, d:]) * y_b).astype(BF16)
    x = x + _dot(mix, wo_ref[...])
    o_ref[...] = _ffn_residual(x, g_ref, wg_ref, wu_ref, wd_ref)


def _merge_call(x2d, za, at, g_mix, w_gate, gate_bias, w_up_a, w_up_b, w_out, g, wg, wu, wd, layer):
    t, d = x2d.shape
    d_rnn, d_att, d_ff = za.shape[1], at.shape[1], wg.shape[-1]
    row = lambda w: pl.BlockSpec((TOKEN_TILE, w), lambda i: (i, 0))
    return pl.pallas_call(
        _merge_kernel,
        out_shape=jax.ShapeDtypeStruct((t, d), F32),
        grid=(t // TOKEN_TILE,),
        in_specs=[row(d), row(d_rnn), row(d_att), _const_spec((1, d)),
                  _layer_spec(layer, (d, 2 * d)), _const_spec((1, 2 * d)),
                  _layer_spec(layer, (d_rnn, d)), _layer_spec(layer, (d_att, d)),
                  _layer_spec(layer, (d, d)), _const_spec((1, d)), _layer_spec(layer, (d, d_ff)),
                  _layer_spec(layer, (d, d_ff)), _layer_spec(layer, (d_ff, d))],
        out_specs=row(d),
        compiler_params=pltpu.CompilerParams(
            dimension_semantics=("parallel",), vmem_limit_bytes=VMEM_LIMIT_RESIDENT),
        name="merge",
    )(x2d, za, at, g_mix, w_gate, gate_bias, w_up_a, w_up_b, w_out, g, wg, wu, wd)


def _segment_permutation():
    seg = TIME_TILE // SUBLANES
    p = np.arange(TIME_TILE)
    perm = np.zeros((TIME_TILE, TIME_TILE), np.float32)
    perm[p, (p % SUBLANES) * seg + p // SUBLANES] = 1.0
    return perm


def _block_diag(w):
    depth, nb, n, _ = w.shape
    tiled = jnp.tile(w.reshape(depth, nb * n, n), (1, 1, nb))
    blk = np.arange(nb * n) // n
    return jnp.where(jnp.asarray(blk[:, None] == blk[None, :]), tiled, 0)


def kernel(x, norm_ffn1, ffn1_w_gate, ffn1_w_up, ffn1_w_down, norm_mix, w_in, gate_bias,
           conv_w, conv_b, rg_w_a, rg_b_a, rg_w_x, rg_b_x, rg_lambda, w_up_a,
           q_gain, k_gain, rel_table, w_up_b, w_out,
           norm_ffn2, ffn2_w_gate, ffn2_w_up, ffn2_w_down):
    bsz, s, d = x.shape
    depth = w_in.shape[0]
    d_rnn = conv_w.shape[-1]
    d_att = w_up_b.shape[1]
    assert q_gain.shape[-1] == HEAD_DIM and rel_table.shape[-1] == 2 * REL_CLIP + 1
    assert (bsz * s) % FFN_TOKEN_TILE == 0 and s % TIME_TILE == 0 and s % (ATT_SUB * ATT_QB) == 0
    assert s % TOKEN_TILE == 0 and d_att == d and TOKEN_TILE % TIME_TILE == 0

    mask = jnp.asarray(_attn_valid_mask())
    perm = _segment_permutation()
    perm, unperm = jnp.asarray(perm, BF16), jnp.asarray(perm.T, BF16)
    row = lambda a: a.reshape(1, -1).astype(F32)
    bf = lambda a: a.astype(BF16)
    rows8 = lambda a: jnp.broadcast_to(a.astype(F32), a.shape[:-2] + (SUBLANES, a.shape[-1]))
    q_scale = HEAD_DIM ** -0.5 * LOG2E

    w_in_b = bf(w_in)
    w_gate = w_in_b[:, :, 2 * d_rnn + 3 * d_att:]
    bias = _bias_call(rel_table.reshape(-1, rel_table.shape[-1]))
    w_ax = bf(jnp.concatenate([_block_diag(rg_w_a), _block_diag(rg_w_x)], axis=2))
    ffn1 = (bf(ffn1_w_gate), bf(ffn1_w_up), bf(ffn1_w_down))
    ffn2 = (bf(ffn2_w_gate), bf(ffn2_w_up), bf(ffn2_w_down))
    w_up_a_b, w_up_b_b, w_out_b = bf(w_up_a), bf(w_up_b), bf(w_out)

    x2d = x.reshape(bsz * s, d)
    for l in range(depth):
        x2d = _ffn_call(x2d, row(norm_ffn1[l]), *ffn1, l)
        q_gain_t = jnp.broadcast_to((q_gain[l].astype(F32) * q_scale)[:, None],
                                    (HEAD_DIM, TOKEN_TILE))
        k_gain2 = jnp.tile(row(k_gain[l]), (1, LANES // HEAD_DIM))
        k, qt, vt, za = _mix_call(
            x2d, row(norm_mix[l]), w_in_b, q_gain_t, k_gain2, perm, unperm,
            rows8(conv_w[l][:, None, :]), rows8(conv_b[l][None, :]), w_ax, row(rg_b_a[l]),
            row(rg_b_x[l]), row(rg_lambda[l]), s, d_rnn, d_att, rg_w_a.shape[1], l)
        at = _attn_call(qt.reshape(bsz, s // ATT_QB, d_att, ATT_QB), k.reshape(bsz, s, d_att),
                        vt.reshape(bsz, s // ATT_QB, d_att, ATT_QB),
                        bias, mask, l)
        x2d = _merge_call(x2d, za, at.reshape(bsz * s, d_att), row(norm_mix[l]), w_gate,
                          row(gate_bias[l]), w_up_a_b, w_up_b_b, w_out_b, row(norm_ffn2[l]),
                          *ffn2, l)
    return x2d.reshape(bsz, s, d)
```

```python
import functools

import numpy as np
import jax
import jax.numpy as jnp
from jax import lax
from jax.experimental import pallas as pl
from jax.experimental.pallas import tpu as pltpu

F32 = jnp.float32
BF16 = jnp.bfloat16

LANES = 128
SUBLANES = 8
MXU_DIM = 256
V7X_VMEM_BYTES = 64 * 1024 * 1024
VMEM_LIMIT_RESIDENT = V7X_VMEM_BYTES * 7 // 8
VMEM_LIMIT_STREAMING = V7X_VMEM_BYTES * 3 // 4

CHUNK = 64
LEFT_CHUNKS = 8
HEAD_DIM = 64
REL_CLIP = 128
CONV_WIDTH = 4
LRU_C = 8.0
EPS = 1e-6
LOG2E = 1.4426950408889634

TOKEN_TILE = 512
FFN_TOKEN_TILE = 1024
FF_TILE = 3 * MXU_DIM
TIME_TILE = 256
ATT_QB = MXU_DIM
ATT_NBLK = LEFT_CHUNKS * CHUNK // ATT_QB + 1
ATT_SUB = 8
ATT_HG = MXU_DIM // HEAD_DIM
ATT_AHEAD = 3
assert (LEFT_CHUNKS * CHUNK) % ATT_QB == 0 and ATT_QB % CHUNK == 0 and TOKEN_TILE % ATT_QB == 0


def _const_spec(shape):
    nd = len(shape)
    return pl.BlockSpec(shape, lambda *_: (0,) * nd, pipeline_mode=pl.Buffered(1))


def _layer_spec(layer, shape):
    nd = len(shape)
    return pl.BlockSpec((None,) + tuple(shape), lambda *_: (layer,) + (0,) * nd,
                        pipeline_mode=pl.Buffered(1))


def _rms_norm(x, g):
    ms = jnp.mean(x * x, axis=-1, keepdims=True)
    return x * lax.rsqrt(ms + EPS) * g


def _dot(a, b):
    return jnp.dot(a, b, preferred_element_type=F32)


def _dot_tt(a, b):
    return lax.dot_general(a, b, (((0,), (1,)), ((), ())), preferred_element_type=F32)


def _ffn_residual(x, g_ref, wg_ref, wu_ref, wd_ref):
    d_ff = wg_ref.shape[1]
    h = _rms_norm(x, g_ref[...]).astype(BF16)
    acc = None
    for c0 in range(0, d_ff, FF_TILE):
        c1 = min(c0 + FF_TILE, d_ff)
        gate = _dot(h, wg_ref[:, c0:c1])
        up = _dot(h, wu_ref[:, c0:c1])
        act = (gate * jax.nn.sigmoid(gate) * up).astype(BF16)
        part = _dot(act, wd_ref[c0:c1, :])
        acc = part if acc is None else acc + part
    return x + 0.5 * acc


def _ffn_kernel(x_ref, g_ref, wg_ref, wu_ref, wd_ref, o_ref):
    o_ref[...] = _ffn_residual(x_ref[...], g_ref, wg_ref, wu_ref, wd_ref)


def _ffn_call(x2d, g, wg, wu, wd, layer):
    t, d = x2d.shape
    d_ff = wg.shape[-1]
    row = pl.BlockSpec((FFN_TOKEN_TILE, d), lambda i: (i, 0))
    return pl.pallas_call(
        _ffn_kernel,
        out_shape=jax.ShapeDtypeStruct((t, d), F32),
        grid=(t // FFN_TOKEN_TILE,),
        in_specs=[row, _const_spec((1, d)), _layer_spec(layer, (d, d_ff)),
                  _layer_spec(layer, (d, d_ff)), _layer_spec(layer, (d_ff, d))],
        out_specs=row,
        compiler_params=pltpu.CompilerParams(
            dimension_semantics=("parallel",), vmem_limit_bytes=VMEM_LIMIT_RESIDENT),
        name="ffn",
    )(x2d, g, wg, wu, wd)


def _head_rms_norm(x, gain2):
    rows, width = x.shape
    low = lax.broadcasted_iota(jnp.int32, (rows, LANES), 1) < HEAD_DIM
    out = []
    for c in range(0, width, LANES):
        xs = x[:, c:c + LANES]
        sq = xs * xs
        ms_lo = jnp.sum(jnp.where(low, sq, 0.0), axis=-1, keepdims=True) * (1.0 / HEAD_DIM)
        ms_hi = jnp.sum(jnp.where(low, 0.0, sq), axis=-1, keepdims=True) * (1.0 / HEAD_DIM)
        r = jnp.where(low, lax.rsqrt(ms_lo + EPS), lax.rsqrt(ms_hi + EPS))
        out.append(xs * r * gain2)
    return jnp.concatenate(out, axis=-1)


def _head_rms_norm_t(xt, gain_t):
    out = []
    for r0 in range(0, xt.shape[0], HEAD_DIM):
        xs = xt[r0:r0 + HEAD_DIM, :]
        ms = jnp.mean(xs * xs, axis=0, keepdims=True)
        out.append(xs * lax.rsqrt(ms + EPS) * gain_t)
    return jnp.concatenate(out, axis=0)


def _rnn_tile_steps(n_blocks, xy_ref, r0, d_rnn, cw_ref, cb_ref, wax_ref, ba_ref, bx_ref, lam_ref,
                    unperm_ref, za_ref, tail_ref, h_ref, u_ref, a_ref, b_ref, z_ref):
    rows, c = TIME_TILE, d_rnn
    seg = rows // SUBLANES
    taps = CONV_WIDTH - 1
    pieces = 4
    sub = lax.broadcasted_iota(jnp.int32, (SUBLANES, c), 0)

    def group(col0, j):
        return xy_ref[r0 + j * SUBLANES:r0 + (j + 1) * SUBLANES, col0:col0 + c]

    last = [group(0, seg - k) for k in range(1, taps + 1)]
    hist = [jnp.where(sub == 0, pltpu.roll(tail_ref[k], 1, axis=0), pltpu.roll(last[k], 1, axis=0))
            for k in range(taps - 1, -1, -1)]
    for k in range(taps):
        tail_ref[k] = last[k]
    for j in range(seg):
        hist.append(last[seg - 1 - j] if seg - 1 - j < taps else group(0, j))
        u = cb_ref[...]
        for t in range(CONV_WIDTH):
            u = u + hist[t] * cw_ref[t]
        u_ref[j * SUBLANES:(j + 1) * SUBLANES, :] = u
        hist.pop(0)
        if (j + 1) % (seg // pieces) == 0:
            yield

    n = c // n_blocks
    decay_rate = (-LRU_C) * jax.nn.softplus(-lam_ref[...])
    for c0 in range(0, c, MXU_DIM):
        c1 = min(c0 + MXU_DIM, c)
        w0 = (c0 // n) * n // LANES * LANES
        w1 = min(-(-(-(-c1 // n) * n) // LANES) * LANES, c)
        u_bf = u_ref[:, w0:w1].astype(BF16)
        r = jax.nn.sigmoid(_dot(u_bf, wax_ref[w0:w1, c0:c1]) + ba_ref[:, c0:c1])
        i = jax.nn.sigmoid(_dot(u_bf, wax_ref[w0:w1, c + c0:c + c1]) + bx_ref[:, c0:c1])
        log_a = r * decay_rate[:, c0:c1]
        a = jnp.exp(log_a)
        a_ref[:, c0:c1] = a
        b_ref[:, c0:c1] = jnp.sqrt(-jnp.tanh(log_a) * (a * a + 1.0)) * (i * u_ref[:, c0:c1])
        yield

    state = jnp.zeros((SUBLANES, c), F32)
    decay = jnp.ones((SUBLANES, c), F32)
    for j in range(seg):
        sl = slice(j * SUBLANES, (j + 1) * SUBLANES)
        a_j = a_ref[sl, :]
        state = a_j * state + b_ref[sl, :]
        decay = a_j * decay
        b_ref[sl, :] = state
        a_ref[sl, :] = decay
    yield

    d = 1
    while d < SUBLANES:
        keep = sub >= d
        decay_prev = jnp.where(keep, pltpu.roll(decay, d, axis=0), 1.0)
        state_prev = jnp.where(keep, pltpu.roll(state, d, axis=0), 0.0)
        state = decay * state_prev + state
        decay = decay * decay_prev
        d *= 2
    ends = decay * h_ref[...] + state
    carry = jnp.where(sub == 0, h_ref[...], pltpu.roll(ends, 1, axis=0))
    h_ref[...] = ends[SUBLANES - 1:SUBLANES, :]

    for j in range(seg):
        sl = slice(j * SUBLANES, (j + 1) * SUBLANES)
        z_ref[sl, :] = jax.nn.gelu(group(c, j)) * (b_ref[sl, :] + a_ref[sl, :] * carry)
        if (j + 1) % (seg // pieces) == 0:
            yield

    za_ref[r0:r0 + rows, :] = _dot(unperm_ref[...], z_ref[...].astype(BF16)).astype(BF16)
    yield


def _interleave(primary, secondary, ratio):
    live = True
    while live:
        live = next(primary, "done") != "done"
        for _ in range(ratio):
            live = (next(secondary, "done") != "done") or live


def _mix_kernel(d_rnn, d_att, n_blocks, steps_per_seq,
                x_ref, g_ref, w_ref, qg_ref, kg_ref, perm_ref, unperm_ref,
                cw_ref, cb_ref, wax_ref, ba_ref, bx_ref, lam_ref,
                k_ref, qt_ref, vt_ref, za_ref,
                xy_ref, tail_ref, h_ref, u_ref, a_ref, b_ref, z_ref):
    taps = CONV_WIDTH - 1
    rows = x_ref.shape[0]
    c_q = 2 * d_rnn

    @pl.when(pl.program_id(0) % steps_per_seq == 0)
    def _():
        tail_ref[...] = jnp.zeros((taps, SUBLANES, d_rnn), F32)
        h_ref[...] = jnp.zeros((1, d_rnn), F32)

    h = _rms_norm(x_ref[...], g_ref[...]).astype(BF16)

    def recurrent_inputs(r0):
        hp = _dot(perm_ref[...], h[r0:r0 + TIME_TILE, :]).astype(BF16)
        for c0 in range(0, c_q, FF_TILE):
            c1 = min(c0 + FF_TILE, c_q)
            xy_ref[r0:r0 + TIME_TILE, c0:c1] = _dot(hp, w_ref[:, c0:c1])
            yield

    def later_inputs():
        for r0 in range(TIME_TILE, rows, TIME_TILE):
            yield from recurrent_inputs(r0)

    def keys():
        for c0 in range(0, d_att, MXU_DIM):
            col = c_q + d_att + c0
            k = _dot(h, w_ref[:, col:col + MXU_DIM])
            k_ref[:, c0:c0 + MXU_DIM] = _head_rms_norm(k, kg_ref[...]).astype(BF16)
            yield

    def transposed(src0, out_ref, norm):
        for c0 in range(0, d_att, MXU_DIM):
            t = _dot_tt(w_ref[:, src0 + c0:src0 + c0 + MXU_DIM], h)
            if norm:
                t = _head_rms_norm_t(t, qg_ref[...])
            t = t.astype(BF16)
            for n in range(out_ref.shape[0]):
                out_ref[n, c0:c0 + MXU_DIM, :] = t[:, n * ATT_QB:(n + 1) * ATT_QB]
            yield

    def projections():
        family = {"x": later_inputs(), "k": keys(), "q": transposed(c_q, qt_ref, True),
                  "v": transposed(c_q + 2 * d_att, vt_ref, False)}
        for name in "kkxxxxkkqqvvvvqq":
            next(family[name])
            yield
        assert all(next(g, "end") == "end" for g in family.values())
        assert rows == 2 * TIME_TILE

    def recurrence():
        for r0 in range(0, rows, TIME_TILE):
            yield from _rnn_tile_steps(
                n_blocks, xy_ref, r0, d_rnn, cw_ref, cb_ref, wax_ref, ba_ref, bx_ref, lam_ref,
                unperm_ref, za_ref, tail_ref, h_ref, u_ref, a_ref, b_ref, z_ref)

    for _ in recurrent_inputs(0):
        pass
    _interleave(projections(), recurrence(), 2)


def _mix_call(x2d, g, w_in, q_gain_t, k_gain2, perm, unperm, conv_w, conv_b, w_ax, b_a, b_x, lam,
              seq_len, d_rnn, d_att, n_blocks, layer):
    t, d = x2d.shape
    nt = TOKEN_TILE // ATT_QB
    c = d_rnn
    row = lambda w: pl.BlockSpec((TOKEN_TILE, w), lambda i: (i, 0))
    sds = lambda w, dt: jax.ShapeDtypeStruct((t, w), dt)
    tiles = pl.BlockSpec((nt, d_att, ATT_QB), lambda i: (i, 0, 0))
    tiles_sds = jax.ShapeDtypeStruct((t // ATT_QB, d_att, ATT_QB), BF16)
    tile = pltpu.VMEM((TIME_TILE, c), F32)
    return pl.pallas_call(
        functools.partial(_mix_kernel, d_rnn, d_att, n_blocks, seq_len // TOKEN_TILE),
        out_shape=(sds(d_att, BF16), tiles_sds, tiles_sds, sds(d_rnn, BF16)),
        grid=(t // TOKEN_TILE,),
        in_specs=[row(d), _const_spec((1, d)), _layer_spec(layer, (d, 2 * d_rnn + 3 * d_att)),
                  _const_spec((HEAD_DIM, TOKEN_TILE)), _const_spec((1, LANES)),
                  _const_spec(perm.shape), _const_spec(unperm.shape),
                  _const_spec((CONV_WIDTH, SUBLANES, c)), _const_spec((SUBLANES, c)),
                  _layer_spec(layer, (c, 2 * c)), _const_spec((1, c)), _const_spec((1, c)),
                  _const_spec((1, c))],
        out_specs=(row(d_att), tiles, tiles, row(d_rnn)),
        scratch_shapes=[pltpu.VMEM((TOKEN_TILE, 2 * c), F32),
                        pltpu.VMEM((CONV_WIDTH - 1, SUBLANES, c), F32),
                        pltpu.VMEM((1, c), F32), tile, tile, tile, tile],
        compiler_params=pltpu.CompilerParams(
            dimension_semantics=("arbitrary",), vmem_limit_bytes=VMEM_LIMIT_RESIDENT),
        name="mix",
    )(x2d, g, w_in, q_gain_t, k_gain2, perm, unperm, conv_w, conv_b, w_ax, b_a, b_x, lam)


def _bias_kernel(base_ref, o_ref):
    for h in range(o_ref.shape[0]):
        for n in range(ATT_NBLK):
            if ATT_QB * n - (ATT_QB - 1) >= REL_CLIP:
                o_ref[h, n] = jnp.broadcast_to(base_ref[h, n][:, :ATT_QB], (ATT_QB, ATT_QB))
                continue
            x = jnp.broadcast_to(base_ref[h, n], (ATT_QB, 2 * ATT_QB))
            o_ref[h, n] = pltpu.roll(x, 0, 1, stride=1, stride_axis=0)[:, :ATT_QB]


def _bias_strips(rel_table):
    nh = rel_table.shape[0]
    j = np.arange(2 * ATT_QB)
    rel = np.where(j < ATT_QB, j, j - 2 * ATT_QB)[None, :] + ATT_QB * np.arange(ATT_NBLK)[:, None]
    idx = np.clip(rel, -REL_CLIP, REL_CLIP) + REL_CLIP
    return (rel_table.astype(F32) * LOG2E)[:, idx].reshape(nh, ATT_NBLK, 1, 2 * ATT_QB)


def _slab_state(n, kc, half):
    per = ATT_QB // CHUNK
    ok = [0 <= per * n + qc - kc <= LEFT_CHUNKS
          for qc in range(half * (LANES // CHUNK), (half + 1) * (LANES // CHUNK))]
    return {(True, True): "all", (False, True): "hi", (True, False): "lo",
            (False, False): "none"}[tuple(ok)]


def _head_query(qt, h):
    lanes = ATT_HG * HEAD_DIM
    r0, r1 = h * HEAD_DIM, (h + 1) * HEAD_DIM
    pieces = []
    if r0:
        pieces.append(jnp.zeros((r0, ATT_QB), BF16))
    pieces.append(qt[r0:r1, :])
    if r1 < lanes:
        pieces.append(jnp.zeros((lanes - r1, ATT_QB), BF16))
    return jnp.concatenate(pieces, axis=0)


def _value_rows(vt_ref, tile, h):
    vt = vt_ref[0, tile, h * HEAD_DIM:(h + 1) * HEAD_DIM, :]
    return jnp.concatenate([vt, jnp.ones((2 * SUBLANES, ATT_QB), BF16)], axis=0)


def _attn_kernel(qt_ref, k_ref, vt_ref, base_ref, mask_ref, o_ref, bias_ref):
    step = pl.program_id(2)

    @pl.when(jnp.logical_and(pl.program_id(1) == 0, step == 0))
    def _():
        _bias_kernel(base_ref, bias_ref)

    halves = ATT_QB // LANES
    kchunks = ATT_QB // CHUNK

    def load_keys(tiles):
        starts = [t * ATT_QB if isinstance(t, int) else pl.multiple_of(t * ATT_QB, ATT_QB)
                  for t in tiles]
        return [k_ref[0, pl.ds(start, ATT_QB), :] for start in starts]

    def head_scores(j, tiles, h):
        qm = _head_query(qt_ref[0, j], h)
        return [_dot(kb, qm) for kb in load_keys(tiles)]

    def column_max(x):
        m = []
        for a in range(halves):
            top = functools.reduce(jnp.maximum, [v for (_, _, aa), v in x.items() if aa == a])
            m.append(jnp.max(top, axis=0, keepdims=True))
        return m

    def interior_scores(j, tiles, h):
        s = head_scores(j, tiles, h)
        hi = lax.broadcasted_iota(jnp.int32, (CHUNK, LANES), 1) >= CHUNK
        x = {}
        for n in range(ATT_NBLK):
            for kc in range(kchunks):
                for a in range(halves):
                    state = _slab_state(n, kc, a)
                    if state == "none":
                        continue
                    rs = slice(kc * CHUNK, (kc + 1) * CHUNK)
                    cs = slice(a * LANES, (a + 1) * LANES)
                    v = s[n][rs, cs] + bias_ref[h, n, rs, cs]
                    if state == "hi":
                        v = jnp.where(hi, v, -1e30)
                    elif state == "lo":
                        v = jnp.where(hi, -1e30, v)
                    x[n, kc, a] = v
        return x, column_max(x)

    def edge_scores(j, gi, tiles, h):
        s = head_scores(j, tiles, h)
        x = {}
        for n in range(ATT_NBLK):
            valid = mask_ref[n] > (0.5 if gi >= n else 2.0)
            xn = jnp.where(valid, s[n] + bias_ref[h, n], -1e30)
            for kc in range(kchunks):
                for a in range(halves):
                    x[n, kc, a] = xn[kc * CHUNK:(kc + 1) * CHUNK, a * LANES:(a + 1) * LANES]
        return x, column_max(x)

    def softmax_pv(tiles, h, x, m):
        o_ext = None
        for n in range(ATT_NBLK):
            rows = []
            for kc in range(kchunks):
                rows.append(jnp.concatenate(
                    [jnp.exp2((x[n, kc, a] - m[a]).astype(BF16)) if (n, kc, a) in x
                     else jnp.zeros((CHUNK, LANES), BF16) for a in range(halves)], axis=1))
            part = _dot(_value_rows(vt_ref, tiles[n], h), jnp.concatenate(rows, axis=0))
            o_ext = part if o_ext is None else o_ext + part
        return o_ext[:HEAD_DIM] * (1.0 / o_ext[HEAD_DIM:HEAD_DIM + 1])

    def run_units(first):
        base = 0 if first else step * ATT_SUB
        tiles = [[(max(j - n, 0) if first else base + j - n) for n in range(ATT_NBLK)]
                 for j in range(ATT_SUB)]

        def scores(j, h):
            if first and j < ATT_NBLK - 1:
                return edge_scores(j, j, tiles[j], h)
            return interior_scores(j, tiles[j], h)

        total = ATT_SUB * ATT_HG
        pending = [scores(t // ATT_HG, t % ATT_HG) for t in range(ATT_AHEAD)]
        outs = []
        for t in range(total):
            j, h = divmod(t, ATT_HG)
            x, m = pending.pop(0)
            if t + ATT_AHEAD < total:
                pending.append(scores((t + ATT_AHEAD) // ATT_HG, (t + ATT_AHEAD) % ATT_HG))
            outs.append(softmax_pv(tiles[j], h, x, m))
            if h == ATT_HG - 1:
                ot = jnp.concatenate(outs, axis=0)
                o_ref[0, j * ATT_QB:(j + 1) * ATT_QB, :] = ot.T.astype(BF16)
                outs = []

    @pl.when(step == 0)
    def _():
        run_units(True)

    @pl.when(step > 0)
    def _():
        run_units(False)


def _attn_call(qt, k, vt, bias, mask, layer):
    bsz, s, d_att = k.shape
    lanes = ATT_HG * HEAD_DIM
    rows = ATT_SUB * ATT_QB
    nt = s // ATT_QB
    return pl.pallas_call(
        _attn_kernel,
        out_shape=jax.ShapeDtypeStruct((bsz, s, d_att), BF16),
        grid=(d_att // lanes, bsz, s // rows),
        in_specs=[pl.BlockSpec((1, ATT_SUB, lanes, ATT_QB), lambda g, b, i: (b, i, g, 0)),
                  pl.BlockSpec((1, s, lanes), lambda g, b, i: (b, 0, g)),
                  pl.BlockSpec((1, nt, lanes, ATT_QB), lambda g, b, i: (b, 0, g, 0)),
                  pl.BlockSpec((ATT_HG, ATT_NBLK, 1, 2 * ATT_QB),
                               lambda g, b, i: (layer * (d_att // lanes) + g, 0, 0, 0)),
                  _const_spec((ATT_NBLK, ATT_QB, ATT_QB))],
        out_specs=pl.BlockSpec((1, rows, lanes), lambda g, b, i: (b, i, g)),
        scratch_shapes=[pltpu.VMEM((ATT_HG, ATT_NBLK, ATT_QB, ATT_QB), F32)],
        compiler_params=pltpu.CompilerParams(
            dimension_semantics=("parallel", "arbitrary", "arbitrary"),
            vmem_limit_bytes=VMEM_LIMIT_STREAMING),
        name="attn",
    )(qt, k, vt, bias, mask)


def _attn_valid_mask():
    per = ATT_QB // CHUNK
    n = np.arange(ATT_NBLK)[:, None, None]
    kc = (np.arange(ATT_QB) // CHUNK)[None, :, None]
    qc = (np.arange(ATT_QB) // CHUNK)[None, None, :]
    dist = per * n + qc - kc
    return ((dist >= 0) & (dist <= LEFT_CHUNKS)).astype(np.float32)


def _merge_kernel(x_ref, za_ref, at_ref, gm_ref, wgate_ref, gbias_ref, wa_ref, wb_ref, wo_ref,
                  g_ref, wg_ref, wu_ref, wd_ref, o_ref):
    d = x_ref.shape[1]
    x = x_ref[...]
    gates = _dot(_rms_norm(x, gm_ref[...]).astype(BF16), wgate_ref[...]) + gbias_ref[...]
    y_a = _dot(za_ref[...], wa_ref[...])
    y_b = _dot(at_ref[...], wb_ref[...])
    mix = (jax.nn.sigmoid(gates[:, :d]) * y_a + jax.nn.sigmoid(gates[:, d:]) * y_b).astype(BF16)
    x = x + _dot(mix, wo_ref[...])
    o_ref[...] = _ffn_residual(x, g_ref, wg_ref, wu_ref, wd_ref)


def _merge_call(x2d, za, at, g_mix, w_gate, gate_bias, w_up_a, w_up_b, w_out, g, wg, wu, wd, layer):
    t, d = x2d.shape
    d_rnn, d_att, d_ff = za.shape[1], at.shape[1], wg.shape[-1]
    row = lambda w: pl.BlockSpec((TOKEN_TILE, w), lambda i: (i, 0))
    return pl.pallas_call(
        _merge_kernel,
        out_shape=jax.ShapeDtypeStruct((t, d), F32),
        grid=(t // TOKEN_TILE,),
        in_specs=[row(d), row(d_rnn), row(d_att), _const_spec((1, d)),
                  _layer_spec(layer, (d, 2 * d)), _const_spec((1, 2 * d)),
                  _layer_spec(layer, (d_rnn, d)), _layer_spec(layer, (d_att, d)),
                  _layer_spec(layer, (d, d)), _const_spec((1, d)), _layer_spec(layer, (d, d_ff)),
                  _layer_spec(layer, (d, d_ff)), _layer_spec(layer, (d_ff, d))],
        out_specs=row(d),
        compiler_params=pltpu.CompilerParams(
            dimension_semantics=("parallel",), vmem_limit_bytes=VMEM_LIMIT_RESIDENT),
        name="merge",
    )(x2d, za, at, g_mix, w_gate, gate_bias, w_up_a, w_up_b, w_out, g, wg, wu, wd)


def _segment_permutation():
    seg = TIME_TILE // SUBLANES
    p = np.arange(TIME_TILE)
    perm = np.zeros((TIME_TILE, TIME_TILE), np.float32)
    perm[p, (p % SUBLANES) * seg + p // SUBLANES] = 1.0
    return perm


def _block_diag(w):
    depth, nb, n, _ = w.shape
    tiled = jnp.tile(w.reshape(depth, nb * n, n), (1, 1, nb))
    blk = np.arange(nb * n) // n
    return jnp.where(jnp.asarray(blk[:, None] == blk[None, :]), tiled, 0)


def kernel(x, norm_ffn1, ffn1_w_gate, ffn1_w_up, ffn1_w_down, norm_mix, w_in, gate_bias,
           conv_w, conv_b, rg_w_a, rg_b_a, rg_w_x, rg_b_x, rg_lambda, w_up_a,
           q_gain, k_gain, rel_table, w_up_b, w_out,
           norm_ffn2, ffn2_w_gate, ffn2_w_up, ffn2_w_down):
    bsz, s, d = x.shape
    depth = w_in.shape[0]
    d_rnn = conv_w.shape[-1]
    d_att = w_up_b.shape[1]
    assert q_gain.shape[-1] == HEAD_DIM and rel_table.shape[-1] == 2 * REL_CLIP + 1
    assert (bsz * s) % FFN_TOKEN_TILE == 0 and s % TIME_TILE == 0 and s % (ATT_SUB * ATT_QB) == 0
    assert s % TOKEN_TILE == 0 and d_att == d and TOKEN_TILE % TIME_TILE == 0

    mask = jnp.asarray(_attn_valid_mask())
    perm = _segment_permutation()
    perm, unperm = jnp.asarray(perm, BF16), jnp.asarray(perm.T, BF16)
    row = lambda a: a.reshape(1, -1).astype(F32)
    bf = lambda a: a.astype(BF16)
    rows8 = lambda a: jnp.broadcast_to(a.astype(F32), a.shape[:-2] + (SUBLANES, a.shape[-1]))
    q_scale = HEAD_DIM ** -0.5 * LOG2E

    w_in_b = bf(w_in)
    w_gate = w_in_b[:, :, 2 * d_rnn + 3 * d_att:]
    bias = _bias_strips(rel_table.reshape(-1, rel_table.shape[-1]))
    w_ax = bf(jnp.concatenate([_block_diag(rg_w_a), _block_diag(rg_w_x)], axis=2))
    ffn1 = (bf(ffn1_w_gate), bf(ffn1_w_up), bf(ffn1_w_down))
    ffn2 = (bf(ffn2_w_gate), bf(ffn2_w_up), bf(ffn2_w_down))
    w_up_a_b, w_up_b_b, w_out_b = bf(w_up_a), bf(w_up_b), bf(w_out)

    x2d = x.reshape(bsz * s, d)
    for l in range(depth):
        x2d = _ffn_call(x2d, row(norm_ffn1[l]), *ffn1, l)
        q_gain_t = jnp.broadcast_to((q_gain[l].astype(F32) * q_scale)[:, None],
                                    (HEAD_DIM, TOKEN_TILE))
        k_gain2 = jnp.tile(row(k_gain[l]), (1, LANES // HEAD_DIM))
        k, qt, vt, za = _mix_call(
            x2d, row(norm_mix[l]), w_in_b, q_gain_t, k_gain2, perm, unperm,
            rows8(conv_w[l][:, None, :]), rows8(conv_b[l][None, :]), w_ax, row(rg_b_a[l]),
            row(rg_b_x[l]), row(rg_lambda[l]), s, d_rnn, d_att, rg_w_a.shape[1], l)
        at = _attn_call(qt.reshape(bsz, s // ATT_QB, d_att, ATT_QB), k.reshape(bsz, s, d_att),
                        vt.reshape(bsz, s // ATT_QB, d_att, ATT_QB),
                        bias, mask, l)
        x2d = _merge_call(x2d, za, at.reshape(bsz * s, d_att), row(norm_mix[l]), w_gate,
                          row(gate_bias[l]), w_up_a_b, w_up_b_b, w_out_b, row(norm_ffn2[l]),
                          *ffn2, l)
    return x2d.reshape(bsz, s, d)
```
